```python
import jax, jax.numpy as jnp
from jax import lax
import numpy as np


D_MODEL = 1024
BATCH = 8
SEQ = 4096
DEPTH = 2

HGRN_HEADS = 4
HGRN_EXPAND = 128
HGRN_HEAD_V = 128
HGRN_KDIM = HGRN_HEADS * HGRN_EXPAND
HGRN_VDIM = HGRN_HEADS * HGRN_HEAD_V
HGRN_CHUNK = 64
MIN_FORGET = 1e-20

MOBA_HEADS = 8
MOBA_HEAD_DIM = 64
MOBA_DIM = MOBA_HEADS * MOBA_HEAD_DIM
MOBA_BLOCK = 256
MOBA_TOPK = 3
MOBA_Q_CHUNK = 32
MASK_VALUE = -1e9

D_FF = 2816
N_BRANCH = 2
NORM_EPS = 1e-6

IN_SIZES = (HGRN_KDIM, HGRN_KDIM, HGRN_VDIM, HGRN_VDIM, MOBA_DIM, MOBA_DIM, MOBA_DIM, N_BRANCH * D_MODEL)
D_IN = int(sum(IN_SIZES))
IN_SPLITS = tuple(int(v) for v in np.cumsum(IN_SIZES)[:-1])

kernel_name = 'hybrid_hgrn2_moba_macaron'


def rmsnorm(x, w):
    xf = x.astype(jnp.float32)
    y = xf * lax.rsqrt(jnp.mean(xf * xf, axis=-1, keepdims=True) + NORM_EPS)
    return (y * w.astype(jnp.float32)).astype(x.dtype)


def swiglu(h, w_gu, w_down):
    gate, up = jnp.split(h @ w_gu, 2, axis=-1)
    return (jax.nn.silu(gate) * up) @ w_down


def alibi_slopes(n_heads):
    return jnp.exp2(-8.0 * jnp.arange(1, n_heads + 1, dtype=jnp.float32) / n_heads)


def hgrn2_mix(q, f_logit, i, g, lower_bound, norm_w):
    B, S, _ = q.shape
    H, dk, dv, C = HGRN_HEADS, HGRN_EXPAND, HGRN_HEAD_V, HGRN_CHUNK
    NC = S // C
    qf = jax.nn.silu(q.astype(jnp.float32))
    fl = f_logit.astype(jnp.float32)
    lb = jnp.clip(lower_bound.astype(jnp.float32), 0.0, 1.0)
    sig = jax.nn.sigmoid(fl)
    f = lb + (1.0 - lb) * sig
    log_f = jnp.log(jnp.maximum(f, MIN_FORGET))
    kf = (1.0 - lb) * (1.0 - sig)
    inp = i.astype(jnp.float32)

    def to_chunks(t, hd):
        return t.reshape(B, NC, C, H, hd).transpose(1, 0, 3, 2, 4)

    causal = jnp.tril(jnp.ones((C, C), dtype=bool))

    def step(state, xs):
        qc, kc, lfc, ic = xs
        b = jnp.cumsum(lfc, axis=2)
        rel = b[:, :, :, None, :] - b[:, :, None, :, :]
        decay = jnp.where(causal[:, :, None], jnp.exp(jnp.minimum(rel, 0.0)), 0.0)
        attn = jnp.einsum('bhtd,bhtsd,bhsd->bhts', qc, decay, kc)
        o = (jnp.einsum('bhts,bhsv->bhtv', attn, ic)
             + jnp.einsum('bhtd,bhdv->bhtv', qc * jnp.exp(b), state))
        b_last = b[:, :, -1:, :]
        state = (jnp.exp(b_last[:, :, 0, :])[..., None] * state
                 + jnp.einsum('bhsd,bhsv->bhdv', kc * jnp.exp(b_last - b), ic))
        return state, o

    state0 = jnp.zeros((B, H, dk, dv), jnp.float32)
    xs = (to_chunks(qf, dk), to_chunks(kf, dk), to_chunks(log_f, dk), to_chunks(inp, dv))
    _, o = lax.scan(step, state0, xs)
    o = o.transpose(1, 0, 3, 2, 4).reshape(B, S, H, dv)
    o = o * lax.rsqrt(jnp.mean(o * o, axis=-1, keepdims=True) + NORM_EPS) * norm_w.astype(jnp.float32)
    o = o * jax.nn.silu(g.astype(jnp.float32).reshape(B, S, H, dv))
    return o.reshape(B, S, HGRN_VDIM).astype(q.dtype)


def moba_mix(q, k, v):
    B, S, _ = q.shape
    H, Dh, BS, QC = MOBA_HEADS, MOBA_HEAD_DIM, MOBA_BLOCK, MOBA_Q_CHUNK
    S_pad = -(-S // BS) * BS
    NB = S_pad // BS
    kt = min(MOBA_TOPK, NB)
    nqc = S_pad // QC
    scale = Dh ** -0.5
    pad = ((0, 0), (0, S_pad - S), (0, 0))

    def heads(t):
        return jnp.pad(t, pad).reshape(B, S_pad, H, Dh).transpose(0, 2, 1, 3)

    qh, kh, vh = heads(q), heads(k), heads(v)
    k_blocks = kh.reshape(B, H, NB, BS, Dh)
    v_blocks = vh.reshape(B, H, NB, BS, Dh)

    k_mean = jnp.mean(k_blocks.astype(jnp.float32), axis=3)
    gate = jnp.einsum('bhsd,bhnd->bhsn', qh.astype(jnp.float32), k_mean)
    q_block = jnp.arange(S_pad) // BS
    past = jnp.arange(NB)[None, :] < q_block[:, None]
    gate = jnp.where(past, gate, MASK_VALUE)
    _, sel = lax.top_k(gate, kt)
    sel_valid = jnp.arange(kt)[None, :] < jnp.minimum(q_block, kt)[:, None]

    q_ch = qh.reshape(B, H, nqc, QC, Dh).transpose(2, 0, 1, 3, 4)
    sel_ch = sel.reshape(B, H, nqc, QC, kt).transpose(2, 0, 1, 3, 4)
    valid_ch = sel_valid.reshape(nqc, QC, kt)
    gather_blocks = jax.vmap(jax.vmap(lambda blk, ix: blk[ix]))
    slopes = alibi_slopes(H)

    def attend(args):
        c, qc, ic, vc = args
        t_pos = c * QC + jnp.arange(QC)
        own_start = (c * QC // BS) * BS
        k_own = lax.dynamic_slice_in_dim(kh, own_start, BS, axis=2)
        v_own = lax.dynamic_slice_in_dim(vh, own_start, BS, axis=2)
        k_sel = gather_blocks(k_blocks, ic)
        v_sel = gather_blocks(v_blocks, ic)
        dist_sel = (t_pos[None, None, :, None, None]
                    - (ic[..., None] * BS + jnp.arange(BS))).astype(jnp.float32)
        s_sel = (jnp.einsum('bhtd,bhtrkd->bhtrk', qc, k_sel, preferred_element_type=jnp.float32) * scale
                 - slopes[None, :, None, None, None] * jnp.abs(dist_sel))
        s_sel = jnp.where(vc[:, :, None], s_sel, MASK_VALUE)
        dist_own = t_pos[:, None] - (own_start + jnp.arange(BS))[None, :]
        s_own = (jnp.einsum('bhtd,bhkd->bhtk', qc, k_own, preferred_element_type=jnp.float32) * scale
                 - slopes[None, :, None, None] * jnp.abs(dist_own).astype(jnp.float32))
        s_own = jnp.where(dist_own >= 0, s_own, MASK_VALUE)
        scores = jnp.concatenate([s_sel.reshape(B, H, QC, kt * BS), s_own], axis=-1)
        p = jax.nn.softmax(scores, axis=-1)
        p_sel = p[..., :kt * BS].reshape(B, H, QC, kt, BS)
        p_own = p[..., kt * BS:]
        out = (jnp.einsum('bhtrk,bhtrkd->bhtd', p_sel, v_sel.astype(jnp.float32))
               + jnp.einsum('bhtk,bhkd->bhtd', p_own, v_own.astype(jnp.float32)))
        return out.astype(q.dtype)

    out = lax.map(attend, (jnp.arange(nqc, dtype=jnp.int32), q_ch, sel_ch, valid_ch))
    out = out.transpose(1, 2, 0, 3, 4).reshape(B, H, S_pad, Dh)[:, :, :S]
    return out.transpose(0, 2, 1, 3).reshape(B, S, MOBA_DIM)


def setup_inputs(seed: int = 0) -> dict:
    key = jax.random.key(seed)
    ks = jax.random.split(key, 16)
    f32 = jnp.float32

    def w(k, shape, fan_in):
        return jax.random.normal(k, shape, f32) * (fan_in ** -0.5)

    def gain(k, shape):
        return 1.0 + 0.01 * jax.random.normal(k, shape, f32)

    return {
        'x': jax.random.normal(ks[0], (BATCH, SEQ, D_MODEL), f32),
        'ffn1_norm': gain(ks[1], (DEPTH, D_MODEL)),
        'ffn1_w_gu': w(ks[2], (DEPTH, D_MODEL, 2 * D_FF), D_MODEL),
        'ffn1_w_down': w(ks[3], (DEPTH, D_FF, D_MODEL), D_FF),
        'mix_norm': gain(ks[4], (DEPTH, D_MODEL)),
        'w_in': w(ks[5], (DEPTH, D_MODEL, D_IN), D_MODEL),
        'hgrn_lb_logits': 0.5 * jax.random.normal(ks[6], (DEPTH, HGRN_KDIM), f32),
        'hgrn_norm': gain(ks[7], (DEPTH, HGRN_HEAD_V)),
        'w_branch_a': w(ks[8], (DEPTH, HGRN_VDIM, D_MODEL), HGRN_VDIM),
        'w_branch_b': w(ks[9], (DEPTH, MOBA_DIM, D_MODEL), MOBA_DIM),
        'w_out': w(ks[10], (DEPTH, D_MODEL, D_MODEL), D_MODEL),
        'ffn2_norm': gain(ks[11], (DEPTH, D_MODEL)),
        'ffn2_w_gu': w(ks[12], (DEPTH, D_MODEL, 2 * D_FF), D_MODEL),
        'ffn2_w_down': w(ks[13], (DEPTH, D_FF, D_MODEL), D_FF),
        'final_norm': gain(ks[14], (D_MODEL,)),
    }


def reference(x, ffn1_norm, ffn1_w_gu, ffn1_w_down, mix_norm, w_in, hgrn_lb_logits, hgrn_norm,
              w_branch_a, w_branch_b, w_out, ffn2_norm, ffn2_w_gu, ffn2_w_down, final_norm):
    lb_sm = jax.nn.softmax(hgrn_lb_logits.astype(jnp.float32), axis=0)
    lower_bounds = jnp.cumsum(lb_sm, axis=0) - lb_sm[0]
    for l in range(DEPTH):
        x = x + 0.5 * swiglu(rmsnorm(x, ffn1_norm[l]), ffn1_w_gu[l], ffn1_w_down[l])
        h = rmsnorm(x, mix_norm[l])
        hq, hf, hi, hg, mq, mk, mv, gates = jnp.split(h @ w_in[l], IN_SPLITS, axis=-1)
        o_a = hgrn2_mix(hq, hf, hi, hg, lower_bounds[l], hgrn_norm[l])
        o_b = moba_mix(mq, mk, mv)
        gate_a, gate_b = jnp.split(jax.nn.sigmoid(gates), 2, axis=-1)
        merged = gate_a * (o_a @ w_branch_a[l]) + gate_b * (o_b @ w_branch_b[l])
        x = x + merged @ w_out[l]
        x = x + 0.5 * swiglu(rmsnorm(x, ffn2_norm[l]), ffn2_w_gu[l], ffn2_w_down[l])
    return rmsnorm(x, final_norm)
```

```python
import functools

import numpy as np
import jax
import jax.numpy as jnp
from jax import lax
from jax.experimental import pallas as pl
from jax.experimental.pallas import tpu as pltpu

F32 = jnp.float32
BF16 = jnp.bfloat16

NORM_EPS = 1e-6
MIN_FORGET = 1e-20
MASK_VALUE = -1e9

HGRN_HEADS = 4
HGRN_DK = 128
MOBA_HEADS = 8
MOBA_HEAD_DIM = 64
MOBA_BLOCK = 256
MOBA_TOPK = 3

V7X_VMEM_LIMIT_BYTES = 56 * 1024 * 1024

HGRN_CHUNK = 128
HGRN_ROWS_PER_STEP = 512
V_AUG_ROWS = 80


def _rmsnorm(x, w):
    return x * lax.rsqrt(jnp.mean(x * x, axis=-1, keepdims=True) + NORM_EPS) * w


def _silu(x):
    return x * jax.nn.sigmoid(x)


def _dot(a, b):
    return jnp.dot(a, b, preferred_element_type=F32)


def _dot_nt(a, b):
    return lax.dot_general(a, b, (((1,), (1,)), ((), ())), preferred_element_type=F32)


def _resident(shape):
    nd = len(shape)
    return pl.BlockSpec(shape, lambda *_: (0,) * nd, pipeline_mode=pl.Buffered(1))


def _params(*sem):
    return pltpu.CompilerParams(dimension_semantics=sem, vmem_limit_bytes=V7X_VMEM_LIMIT_BYTES)


def _ffn_kernel(x_ref, nw_ref, wg_ref, wu_ref, wd_ref, *rest, final):
    o_ref = rest[-1]
    x = x_ref[...]
    h = _rmsnorm(x, nw_ref[...]).astype(BF16)
    g = _dot(h, wg_ref[...])
    u = _dot(h, wu_ref[...])
    a = (_silu(g) * u).astype(BF16)
    y = x + 0.5 * _dot(a, wd_ref[...])
    if final:
        y = _rmsnorm(y, rest[0][...])
    o_ref[...] = y


def _ffn(x, nw, wg, wu, wd, final_w=None, tm=256):
    n, d = x.shape
    ff = wg.shape[1]
    row = pl.BlockSpec((tm, d), lambda i: (i, 0))
    in_specs = [row, _resident((1, d)), _resident((d, ff)), _resident((d, ff)), _resident((ff, d))]
    args = [x, nw.reshape(1, d), wg, wu, wd]
    if final_w is not None:
        in_specs.append(_resident((1, d)))
        args.append(final_w.reshape(1, d))
    return pl.pallas_call(
        functools.partial(_ffn_kernel, final=final_w is not None),
        grid=(n // tm,),
        in_specs=in_specs,
        out_specs=row,
        out_shape=jax.ShapeDtypeStruct((n, d), F32),
        compiler_params=_params("parallel"),
        name="ffn",
    )(*args)


def _inproj_kernel(x_ref, nw_ref, wh_ref, wm_ref, wg_ref, h_ref, m_ref, g_ref):
    h = _rmsnorm(x_ref[...], nw_ref[...]).astype(BF16)
    h_ref[...] = _dot(h, wh_ref[...])
    m_ref[...] = _dot(h, wm_ref[...]).astype(BF16)
    g_ref[...] = _dot(h, wg_ref[...])


def _inproj(x, nw, wh, wm, wg, tm=256):
    n, d = x.shape
    dh, dm, dg = wh.shape[1], wm.shape[1], wg.shape[1]
    return pl.pallas_call(
        _inproj_kernel,
        grid=(n // tm,),
        in_specs=[pl.BlockSpec((tm, d), lambda i: (i, 0)), _resident((1, d)),
                  _resident((d, dh)), _resident((d, dm)), _resident((d, dg))],
        out_specs=[pl.BlockSpec((tm, dh), lambda i: (i, 0)),
                   pl.BlockSpec((tm, dm), lambda i: (i, 0)),
                   pl.BlockSpec((tm, dg), lambda i: (i, 0))],
        out_shape=[jax.ShapeDtypeStruct((n, dh), F32),
                   jax.ShapeDtypeStruct((n, dm), BF16),
                   jax.ShapeDtypeStruct((n, dg), F32)],
        compiler_params=_params("parallel"),
        name="inproj",
    )(x, nw.reshape(1, d), wh, wm, wg)


def _hgrn_constants(c):
    t = np.arange(c)
    sp = t[None, :]
    tt = t[:, None]
    rows = [sp <= tt, sp > tt]
    masks, upper = [], []
    m = 1
    while m < c:
        up = (t // m) % 2 == 1
        bnd = ((t // (2 * m)) * 2 * m + m - 1)[:, None]
        rows.append(np.where(up[:, None], (sp > bnd) & (sp <= tt), (sp > tt) & (sp <= bnd)))
        same = (tt // (2 * m)) == (sp // (2 * m))
        masks.append(up[:, None] & (~up)[None, :] & same)
        upper.append(np.broadcast_to(up[:, None], (c, HGRN_DK)))
        m *= 2
    w = np.concatenate(rows, axis=0).astype(np.float32)
    return (jnp.asarray(w, BF16), jnp.asarray(np.stack(masks), F32),
            jnp.asarray(np.stack(upper), F32))


def _hgrn_kernel(lbl_ref, nw_ref, w_ref, lm_ref, up_ref, h_ref, o_ref, st_ref, *, layer, n_levels):
    c = HGRN_CHUNK
    dk = HGRN_DK
    kdim = HGRN_HEADS * dk

    @pl.when(pl.program_id(1) == 0)
    def _():
        st_ref[...] = jnp.zeros_like(st_ref)

    logits = lbl_ref[...]
    e = jnp.exp(logits - jnp.max(logits, axis=0, keepdims=True))
    sm = e / jnp.sum(e, axis=0, keepdims=True)
    cum = sm[0:1, :]
    for j in range(1, layer + 1):
        cum = cum + sm[j:j + 1, :]
    lb_all = jnp.clip(cum - sm[0:1, :], 0.0, 1.0)
    w = w_ref[...]
    nw = nw_ref[...]

    def chunk(ci, carry):
        r0 = pl.multiple_of(ci * c, c)
        for h in range(HGRN_HEADS):
            lanes = slice(h * dk, (h + 1) * dk)
            lb = lb_all[:, lanes]
            q = _silu(h_ref[pl.ds(r0, c), h * dk:(h + 1) * dk])
            sig = jax.nn.sigmoid(h_ref[pl.ds(r0, c), kdim + h * dk:kdim + (h + 1) * dk])
            inp = h_ref[pl.ds(r0, c), 2 * kdim + h * dk:2 * kdim + (h + 1) * dk]
            og = h_ref[pl.ds(r0, c), 3 * kdim + h * dk:3 * kdim + (h + 1) * dk]
            f = lb + (1.0 - lb) * sig
            g = jnp.log(jnp.maximum(f, MIN_FORGET))
            k = (1.0 - lb) * (1.0 - sig)

            g_hi = g.astype(BF16)
            g_lo = (g - g_hi.astype(F32)).astype(BF16)
            r = _dot(w, jnp.concatenate([g_hi, g_lo], axis=1))
            x = jnp.exp(r[:, :dk] + r[:, dk:])

            qd = (q * x[0:c]).astype(BF16)
            kd = (k * x[c:2 * c]).astype(BF16)
            decay_all = x[c - 1:c, :]
            dqk = q - k
            a = jnp.zeros((c, c), F32)
            for l in range(n_levels):
                z = ((k + up_ref[l] * dqk) * x[(2 + l) * c:(3 + l) * c]).astype(BF16)
                a = a + lm_ref[l] * _dot_nt(z, z)
            inp_b = inp.astype(BF16)
            st = st_ref[h]
            o = (_dot(a.astype(BF16), inp_b) + _dot_nt(qd, st.astype(BF16))
                 + jnp.sum(q * k, axis=-1, keepdims=True) * inp)
            st_ref[h] = st * decay_all + _dot(inp.T.astype(BF16), kd)

            o = o * lax.rsqrt(jnp.mean(o * o, axis=-1, keepdims=True) + NORM_EPS) * nw
            o_ref[pl.ds(r0, c), h * dk:(h + 1) * dk] = (o * _silu(og)).astype(o_ref.dtype)
        return carry

    lax.fori_loop(0, h_ref.shape[0] // c, chunk, 0)


def _hgrn(h4, lb_logits, norm_w, layer, batch, seq):
    n = h4.shape[0]
    c = HGRN_CHUNK
    tb = HGRN_ROWS_PER_STEP
    steps = seq // tb
    w, lm, up = _hgrn_constants(c)
    n_levels = lm.shape[0]
    vdim = HGRN_HEADS * HGRN_DK
    return pl.pallas_call(
        functools.partial(_hgrn_kernel, layer=layer, n_levels=n_levels),
        grid=(batch, steps),
        in_specs=[_resident(lb_logits.shape), _resident((1, HGRN_DK)), _resident(w.shape),
                  _resident(lm.shape), _resident(up.shape),
                  pl.BlockSpec((tb, h4.shape[1]), lambda b, s: (b * steps + s, 0))],
        out_specs=pl.BlockSpec((tb, vdim), lambda b, s: (b * steps + s, 0)),
        out_shape=jax.ShapeDtypeStruct((n, vdim), BF16),
        scratch_shapes=[pltpu.VMEM((HGRN_HEADS, HGRN_DK, HGRN_DK), F32)],
        compiler_params=_params("parallel", "arbitrary"),
        name="hgrn",
    )(lb_logits, norm_w.reshape(1, HGRN_DK), w, lm, up, h4)


def _moba_kernel(slope_ref, q_ref, k_ref, vt_ref, pb_ref, pd_ref, o_ref, km_ref, sel_ref, *, nb):
    bs = MOBA_BLOCK
    dh = MOBA_HEAD_DIM
    hp = pl.program_id(1)
    i = pl.program_id(2)

    @pl.when(i == 0)
    def _():
        for j in range(nb):
            km_ref[j:j + 1, :] = jnp.mean(k_ref[j * bs:(j + 1) * bs, :].astype(F32), axis=0, keepdims=True)

    q = q_ref[...]
    lane = lax.broadcasted_iota(jnp.int32, (1, 2 * dh), 1)
    head_lanes = [lane < dh, lane >= dh]

    km = km_ref[...]
    km2 = jnp.concatenate([jnp.where(head_lanes[0], km, 0.0), jnp.where(head_lanes[1], km, 0.0)], axis=0)
    p0 = km2.astype(BF16)
    r1 = km2 - p0.astype(F32)
    p1 = r1.astype(BF16)
    p2 = (r1 - p1.astype(F32)).astype(BF16)
    gate2 = _dot_nt(p0, q) + _dot_nt(p1, q) + _dot_nt(p2, q)

    jidx = lax.broadcasted_iota(jnp.int32, (nb, bs), 0)
    past = jidx < i
    kt = min(MOBA_TOPK, nb)
    for a in range(2):
        gm = jnp.where(past, gate2[a * nb:(a + 1) * nb, :], MASK_VALUE)
        cnt = jnp.zeros((nb, bs), F32)
        for jp in range(nb):
            row = gm[jp:jp + 1, :]
            beats = (row > gm) | ((row == gm) & (jp < jidx))
            cnt = cnt + jnp.where(beats, 1.0, 0.0) * (jp < i).astype(F32)
        sel_ref[a] = jnp.where(past & (cnt < kt), 1.0, 0.0)

    scale = dh ** -0.5
    qm = [jnp.where(head_lanes[a], q, jnp.zeros_like(q)) * jnp.asarray(scale, BF16) for a in range(2)]
    slopes = [slope_ref[2 * hp + a] for a in range(2)]

    def attend(j, bias_ref, sel_rows, carry):
        k_j = k_ref[pl.ds(pl.multiple_of(j * bs, bs), bs), :]
        jf = j.astype(F32) * float(bs)
        out = []
        for a in range(2):
            m_old, acc = carry[a]
            s = _dot_nt(k_j, qm[a]) + bias_ref[a]
            cj = slopes[a] * jf
            m_blk = jnp.max(s, axis=0, keepdims=True) + cj
            if sel_rows is not None:
                m_blk = jnp.where(sel_rows[a], m_blk, -jnp.inf)
            m_new = jnp.maximum(m_old, m_blk)
            alpha = jnp.exp(m_old - m_new)
            shift = m_new - cj
            if sel_rows is not None:
                shift = jnp.where(sel_rows[a], shift, jnp.inf)
            p = jnp.exp(s - shift).astype(BF16)
            acc = alpha * acc + _dot(vt_ref[0, a, j], p)
            out.append((m_new, acc))
        return tuple(out)

    init = tuple((jnp.full((1, bs), -jnp.inf, F32), jnp.zeros((V_AUG_ROWS, bs), F32)) for _ in range(2))
    carry = attend(i, pd_ref, None, init)

    def past_block(j, carry):
        sel_rows = [sel_ref[a, pl.ds(j, 1), :] > 0.5 for a in range(2)]
        return attend(j, pb_ref, sel_rows, carry)

    carry = lax.fori_loop(0, i, past_block, carry)

    outs = [carry[a][1][0:dh] / carry[a][1][dh:dh + 1] for a in range(2)]
    o_ref[...] = jnp.concatenate(outs, axis=0).T.astype(o_ref.dtype)


def _moba(m3, batch, seq):
    n = m3.shape[0]
    bs, dh, nh = MOBA_BLOCK, MOBA_HEAD_DIM, MOBA_HEADS
    nb = seq // bs
    dim = nh * dh
    v = m3[:, 2 * dim:3 * dim].reshape(batch, nb, bs, nh, dh).transpose(0, 3, 1, 4, 2)
    vt = jnp.concatenate([v, jnp.ones((batch, nh, nb, V_AUG_ROWS - dh, bs), BF16)], axis=3)
    slopes = jnp.exp2(-8.0 * jnp.arange(1, nh + 1, dtype=F32) / nh)
    key_pos = jnp.arange(bs, dtype=F32)[:, None]
    pos_bias = slopes[:, None, None] * jnp.broadcast_to(key_pos, (bs, bs))[None]
    causal = jnp.arange(bs)[:, None] <= jnp.arange(bs)[None, :]
    pos_bias_diag = jnp.where(causal[None], pos_bias, -jnp.inf)
    qcol, kcol = 0, dim // (2 * dh)
    return pl.pallas_call(
        functools.partial(_moba_kernel, nb=nb),
        grid=(batch, nh // 2, nb),
        in_specs=[pl.BlockSpec(memory_space=pltpu.SMEM),
                  pl.BlockSpec((bs, 2 * dh), lambda b, hp, i: (b * nb + i, qcol + hp)),
                  pl.BlockSpec((seq, 2 * dh), lambda b, hp, i: (b, kcol + hp)),
                  pl.BlockSpec((1, 2, nb, V_AUG_ROWS, bs), lambda b, hp, i: (b, hp, 0, 0, 0)),
                  pl.BlockSpec((2, bs, bs), lambda b, hp, i: (hp, 0, 0)),
                  pl.BlockSpec((2, bs, bs), lambda b, hp, i: (hp, 0, 0))],
        out_specs=pl.BlockSpec((bs, 2 * dh), lambda b, hp, i: (b * nb + i, hp)),
        out_shape=jax.ShapeDtypeStruct((n, dim), BF16),
        scratch_shapes=[pltpu.VMEM((nb, 2 * dh), F32), pltpu.VMEM((2, nb, bs), F32)],
        compiler_params=_params("parallel", "parallel", "arbitrary"),
        name="moba",
    )(slopes, m3, m3, vt, pos_bias, pos_bias_diag)


def _merge_kernel(x_ref, oa_ref, ob_ref, g_ref, wa_ref, wb_ref, wo_ref, o_ref):
    d = x_ref.shape[1]
    ga = jax.nn.sigmoid(g_ref[:, 0:d])
    gb = jax.nn.sigmoid(g_ref[:, d:2 * d])
    merged = ga * _dot(oa_ref[...], wa_ref[...]) + gb * _dot(ob_ref[...], wb_ref[...])
    o_ref[...] = x_ref[...] + _dot(merged.astype(BF16), wo_ref[...])


def _merge(x, oa, ob, gates, wa, wb, wo, tm=512):
    n, d = x.shape

    def row(width):
        return pl.BlockSpec((tm, width), lambda i: (i, 0))

    return pl.pallas_call(
        _merge_kernel,
        grid=(n // tm,),
        in_specs=[row(d), row(oa.shape[1]), row(ob.shape[1]), row(gates.shape[1]),
                  _resident(wa.shape), _resident(wb.shape), _resident(wo.shape)],
        out_specs=row(d),
        out_shape=jax.ShapeDtypeStruct((n, d), F32),
        compiler_params=_params("parallel"),
        name="merge",
    )(x, oa, ob, gates, wa, wb, wo)


def kernel(x, ffn1_norm, ffn1_w_gu, ffn1_w_down, mix_norm, w_in, hgrn_lb_logits, hgrn_norm,
           w_branch_a, w_branch_b, w_out, ffn2_norm, ffn2_w_gu, ffn2_w_down, final_norm):
    batch, seq, d = x.shape
    depth = ffn1_norm.shape[0]
    ff = ffn1_w_down.shape[1]
    hdim = 4 * HGRN_HEADS * HGRN_DK
    mdim = 3 * MOBA_HEADS * MOBA_HEAD_DIM
    xf = x.reshape(batch * seq, d)
    for l in range(depth):
        wgu = ffn1_w_gu[l].astype(BF16)
        xf = _ffn(xf, ffn1_norm[l], wgu[:, :ff], wgu[:, ff:], ffn1_w_down[l].astype(BF16))
        wi = w_in[l].astype(BF16)
        h4, m3, gates = _inproj(xf, mix_norm[l], wi[:, :hdim], wi[:, hdim:hdim + mdim], wi[:, hdim + mdim:])
        oa = _hgrn(h4, hgrn_lb_logits, hgrn_norm[l], l, batch, seq)
        ob = _moba(m3, batch, seq)
        xf = _merge(xf, oa, ob, gates, w_branch_a[l].astype(BF16), w_branch_b[l].astype(BF16),
                    w_out[l].astype(BF16))
        wgu = ffn2_w_gu[l].astype(BF16)
        xf = _ffn(xf, ffn2_norm[l], wgu[:, :ff], wgu[:, ff:], ffn2_w_down[l].astype(BF16),
                  final_w=final_norm if l == depth - 1 else None)
    return xf.reshape(batch, seq, d)
```

```python
import functools

import numpy as np
import jax
import jax.numpy as jnp
from jax import lax
from jax.experimental import pallas as pl
from jax.experimental.pallas import tpu as pltpu

F32 = jnp.float32
BF16 = jnp.bfloat16

NORM_EPS = 1e-6
MIN_FORGET = 1e-20
MASK_VALUE = -1e9

HGRN_HEADS = 4
HGRN_DK = 128
MOBA_HEADS = 8
MOBA_HEAD_DIM = 64
MOBA_BLOCK = 256
MOBA_TOPK = 3

V7X_VMEM_LIMIT_BYTES = 56 * 1024 * 1024

HGRN_CHUNK = 128
HGRN_ROWS_PER_STEP = 512
V_AUG_ROWS = 80


def _rmsnorm(x, w):
    return x * lax.rsqrt(jnp.mean(x * x, axis=-1, keepdims=True) + NORM_EPS) * w


def _silu(x):
    return x * jax.nn.sigmoid(x)


def _dot(a, b):
    return jnp.dot(a, b, preferred_element_type=F32)


def _dot_nt(a, b):
    return lax.dot_general(a, b, (((1,), (1,)), ((), ())), preferred_element_type=F32)


def _resident(shape):
    nd = len(shape)
    return pl.BlockSpec(shape, lambda *_: (0,) * nd, pipeline_mode=pl.Buffered(1))


def _params(*sem):
    return pltpu.CompilerParams(dimension_semantics=sem, vmem_limit_bytes=V7X_VMEM_LIMIT_BYTES)


def _ffn_kernel(x_ref, nw_ref, wg_ref, wu_ref, wd_ref, *rest, final):
    o_ref = rest[-1]
    x = x_ref[...]
    h = _rmsnorm(x, nw_ref[...]).astype(BF16)
    g = _dot(h, wg_ref[...])
    u = _dot(h, wu_ref[...])
    a = (_silu(g) * u).astype(BF16)
    y = x + 0.5 * _dot(a, wd_ref[...])
    if final:
        y = _rmsnorm(y, rest[0][...])
    o_ref[...] = y


def _ffn(x, nw, wg, wu, wd, final_w=None, tm=256):
    n, d = x.shape
    ff = wg.shape[1]
    row = pl.BlockSpec((tm, d), lambda i: (i, 0))
    in_specs = [row, _resident((1, d)), _resident((d, ff)), _resident((d, ff)), _resident((ff, d))]
    args = [x, nw.reshape(1, d), wg, wu, wd]
    if final_w is not None:
        in_specs.append(_resident((1, d)))
        args.append(final_w.reshape(1, d))
    return pl.pallas_call(
        functools.partial(_ffn_kernel, final=final_w is not None),
        grid=(n // tm,),
        in_specs=in_specs,
        out_specs=row,
        out_shape=jax.ShapeDtypeStruct((n, d), F32),
        compiler_params=_params("parallel"),
        name="ffn",
    )(*args)


def _inproj_kernel(x_ref, nw_ref, wh_ref, wm_ref, wg_ref, h_ref, m_ref, g_ref):
    h = _rmsnorm(x_ref[...], nw_ref[...]).astype(BF16)
    h_ref[...] = _dot(h, wh_ref[...])
    m_ref[...] = _dot(h, wm_ref[...]).astype(BF16)
    g_ref[...] = _dot(h, wg_ref[...])


def _inproj(x, nw, wh, wm, wg, tm=256):
    n, d = x.shape
    dh, dm, dg = wh.shape[1], wm.shape[1], wg.shape[1]
    return pl.pallas_call(
        _inproj_kernel,
        grid=(n // tm,),
        in_specs=[pl.BlockSpec((tm, d), lambda i: (i, 0)), _resident((1, d)),
                  _resident((d, dh)), _resident((d, dm)), _resident((d, dg))],
        out_specs=[pl.BlockSpec((tm, dh), lambda i: (i, 0)),
                   pl.BlockSpec((tm, dm), lambda i: (i, 0)),
                   pl.BlockSpec((tm, dg), lambda i: (i, 0))],
        out_shape=[jax.ShapeDtypeStruct((n, dh), F32),
                   jax.ShapeDtypeStruct((n, dm), BF16),
                   jax.ShapeDtypeStruct((n, dg), F32)],
        compiler_params=_params("parallel"),
        name="inproj",
    )(x, nw.reshape(1, d), wh, wm, wg)


def _hgrn_constants(c):
    t = np.arange(c)
    sp = t[None, :]
    tt = t[:, None]
    rows = [sp <= tt, sp > tt]
    masks, upper = [], []
    m = 1
    while m < c:
        up = (t // m) % 2 == 1
        bnd = ((t // (2 * m)) * 2 * m + m - 1)[:, None]
        rows.append(np.where(up[:, None], (sp > bnd) & (sp <= tt), (sp > tt) & (sp <= bnd)))
        same = (tt // (2 * m)) == (sp // (2 * m))
        masks.append(up[:, None] & (~up)[None, :] & same)
        upper.append(np.broadcast_to(up[:, None], (c, HGRN_DK)))
        m *= 2
    w = np.concatenate(rows, axis=0).astype(np.float32)
    return (jnp.asarray(w, BF16), jnp.asarray(np.stack(masks), F32),
            jnp.asarray(np.stack(upper), F32))


def _hgrn_kernel(lbl_ref, nw_ref, w_ref, lm_ref, up_ref, h_ref, o_ref, st_ref, *, layer, n_levels):
    c = HGRN_CHUNK
    dk = HGRN_DK
    kdim = HGRN_HEADS * dk

    @pl.when(pl.program_id(1) == 0)
    def _():
        st_ref[...] = jnp.zeros_like(st_ref)

    logits = lbl_ref[...]
    e = jnp.exp(logits - jnp.max(logits, axis=0, keepdims=True))
    sm = e / jnp.sum(e, axis=0, keepdims=True)
    cum = sm[0:1, :]
    for j in range(1, layer + 1):
        cum = cum + sm[j:j + 1, :]
    lb_all = jnp.clip(cum - sm[0:1, :], 0.0, 1.0)
    w = w_ref[...]
    nw = nw_ref[...]

    def chunk(ci, carry):
        r0 = pl.multiple_of(ci * c, c)
        for h in range(HGRN_HEADS):
            lanes = slice(h * dk, (h + 1) * dk)
            lb = lb_all[:, lanes]
            q = _silu(h_ref[pl.ds(r0, c), h * dk:(h + 1) * dk])
            sig = jax.nn.sigmoid(h_ref[pl.ds(r0, c), kdim + h * dk:kdim + (h + 1) * dk])
            inp = h_ref[pl.ds(r0, c), 2 * kdim + h * dk:2 * kdim + (h + 1) * dk]
            og = h_ref[pl.ds(r0, c), 3 * kdim + h * dk:3 * kdim + (h + 1) * dk]
            f = lb + (1.0 - lb) * sig
            g = jnp.log(jnp.maximum(f, MIN_FORGET))
            k = (1.0 - lb) * (1.0 - sig)

            g_hi = g.astype(BF16)
            g_lo = (g - g_hi.astype(F32)).astype(BF16)
            r = _dot(w, jnp.concatenate([g_hi, g_lo], axis=1))
            x = jnp.exp(r[:, :dk] + r[:, dk:])

            qd = (q * x[0:c]).astype(BF16)
            kd = (k * x[c:2 * c]).astype(BF16)
            decay_all = x[c - 1:c, :]
            dqk = q - k
            a = jnp.zeros((c, c), F32)
            for l in range(n_levels):
                z = ((k + up_ref[l] * dqk) * x[(2 + l) * c:(3 + l) * c]).astype(BF16)
                a = a + lm_ref[l] * _dot_nt(z, z)
            inp_b = inp.astype(BF16)
            st = st_ref[h]
            o = (_dot(a.astype(BF16), inp_b) + _dot_nt(qd, st.astype(BF16))
                 + jnp.sum(q * k, axis=-1, keepdims=True) * inp)
            st_ref[h] = st * decay_all + _dot(inp.T.astype(BF16), kd)

            o = o * lax.rsqrt(jnp.mean(o * o, axis=-1, keepdims=True) + NORM_EPS) * nw
            o_ref[pl.ds(r0, c), h * dk:(h + 1) * dk] = (o * _silu(og)).astype(o_ref.dtype)
        return carry

    lax.fori_loop(0, h_ref.shape[0] // c, chunk, 0)


def _hgrn(h4, lb_logits, norm_w, layer, batch, seq):
    n = h4.shape[0]
    c = HGRN_CHUNK
    tb = HGRN_ROWS_PER_STEP
    steps = seq // tb
    w, lm, up = _hgrn_constants(c)
    n_levels = lm.shape[0]
    vdim = HGRN_HEADS * HGRN_DK
    return pl.pallas_call(
        functools.partial(_hgrn_kernel, layer=layer, n_levels=n_levels),
        grid=(batch, steps),
        in_specs=[_resident(lb_logits.shape), _resident((1, HGRN_DK)), _resident(w.shape),
                  _resident(lm.shape), _resident(up.shape),
                  pl.BlockSpec((tb, h4.shape[1]), lambda b, s: (b * steps + s, 0))],
        out_specs=pl.BlockSpec((tb, vdim), lambda b, s: (b * steps + s, 0)),
        out_shape=jax.ShapeDtypeStruct((n, vdim), BF16),
        scratch_shapes=[pltpu.VMEM((HGRN_HEADS, HGRN_DK, HGRN_DK), F32)],
        compiler_params=_params("parallel", "arbitrary"),
        name="hgrn",
    )(lb_logits, norm_w.reshape(1, HGRN_DK), w, lm, up, h4)


def _moba_kernel(slope_ref, q_ref, k_ref, vt_ref, pb_ref, o_ref, km_ref, sel_ref, s_ref, *, nb):
    bs = MOBA_BLOCK
    dh = MOBA_HEAD_DIM
    hp = pl.program_id(1)
    i = pl.program_id(2)

    @pl.when(i == 0)
    def _():
        for j in range(nb):
            km_ref[j:j + 1, :] = jnp.mean(k_ref[j * bs:(j + 1) * bs, :].astype(F32), axis=0, keepdims=True)

    q = q_ref[...]
    lane = lax.broadcasted_iota(jnp.int32, (1, 2 * dh), 1)
    head_lanes = [lane < dh, lane >= dh]

    km = km_ref[...]
    km2 = jnp.concatenate([jnp.where(head_lanes[0], km, 0.0), jnp.where(head_lanes[1], km, 0.0)], axis=0)
    p0 = km2.astype(BF16)
    r1 = km2 - p0.astype(F32)
    p1 = r1.astype(BF16)
    p2 = (r1 - p1.astype(F32)).astype(BF16)
    gate2 = _dot_nt(p0, q) + _dot_nt(p1, q) + _dot_nt(p2, q)

    jidx = lax.broadcasted_iota(jnp.int32, (nb, bs), 0)
    past = jidx < i
    kt = min(MOBA_TOPK, nb)
    for a in range(2):
        gm = jnp.where(past, gate2[a * nb:(a + 1) * nb, :], MASK_VALUE)
        cnt = jnp.zeros((nb, bs), F32)
        for jp in range(nb):
            row = gm[jp:jp + 1, :]
            beats = (row > gm) | ((row == gm) & (jp < jidx))
            cnt = cnt + jnp.where(beats, 1.0, 0.0) * (jp < i).astype(F32)
        sel_ref[a] = jnp.where((past & (cnt < kt)) | (jidx == i), 1.0, 0.0)

    scale = dh ** -0.5
    qm = [jnp.where(head_lanes[a], q, jnp.zeros_like(q)) * jnp.asarray(scale, BF16) for a in range(2)]
    slopes = [slope_ref[2 * hp + a] for a in range(2)]

    def block_ids(t):
        ids = []
        for u in range(2):
            b = i - 2 * t - u
            ids.append((jnp.maximum(b, 0), (b >= 0).astype(F32)))
        return ids

    def scores(t):
        s_all, stats = [], []
        for u, (bc, valid) in enumerate(block_ids(t)):
            k_b = k_ref[pl.ds(pl.multiple_of(bc * bs, bs), bs), :]
            diag = (bc == i).astype(jnp.int32)
            for a in range(2):
                s = _dot_nt(k_b, qm[a]) + pb_ref[diag, a]
                chosen = sel_ref[a, pl.ds(bc, 1), :] * valid > 0.5
                off = jnp.where(chosen, slopes[a] * (bc.astype(F32) * float(bs)), -jnp.inf)
                s_all.append(s)
                stats.append((jnp.max(s, axis=0, keepdims=True) + off, off))
        return s_all, tuple(stats)

    def step(t, carry):
        stats, state = carry
        s_cur = [s_ref[ua] for ua in range(4)]
        s_nxt, stats_nxt = scores(t + 1)
        ids = block_ids(t)
        out = []
        for a in range(2):
            m_old, acc = state[a]
            m_new = jnp.maximum(m_old, jnp.maximum(stats[a][0], stats[2 + a][0]))
            p = [jnp.exp(s_cur[2 * u + a] - (m_new - stats[2 * u + a][1])).astype(BF16) for u in range(2)]
            vt = jnp.concatenate([vt_ref[0, a, ids[u][0]] for u in range(2)], axis=1)
            pv = _dot(vt, jnp.concatenate(p, axis=0))
            out.append((m_new, jnp.exp(m_old - m_new) * acc + pv))
        for ua in range(4):
            s_ref[ua] = s_nxt[ua]
        return stats_nxt, tuple(out)

    s0, stats0 = scores(0)
    for ua in range(4):
        s_ref[ua] = s0[ua]
    init = tuple((jnp.full((1, bs), -jnp.inf, F32), jnp.zeros((V_AUG_ROWS, bs), F32)) for _ in range(2))
    _, carry = lax.fori_loop(0, (i + 2) // 2, step, (stats0, init))

    outs = [carry[a][1][0:dh] / carry[a][1][dh:dh + 1] for a in range(2)]
    o_ref[...] = jnp.concatenate(outs, axis=0).T.astype(o_ref.dtype)


def _moba(m3, batch, seq):
    n = m3.shape[0]
    bs, dh, nh = MOBA_BLOCK, MOBA_HEAD_DIM, MOBA_HEADS
    nb = seq // bs
    dim = nh * dh
    v = m3[:, 2 * dim:3 * dim].reshape(batch, nb, bs, nh, dh).transpose(0, 3, 1, 4, 2)
    vt = jnp.concatenate([v, jnp.ones((batch, nh, nb, V_AUG_ROWS - dh, bs), BF16)], axis=3)
    slopes = jnp.exp2(-8.0 * jnp.arange(1, nh + 1, dtype=F32) / nh)
    key_pos = jnp.arange(bs, dtype=F32)[:, None]
    pos_bias = slopes[:, None, None] * jnp.broadcast_to(key_pos, (bs, bs))[None]
    causal = jnp.arange(bs)[:, None] <= jnp.arange(bs)[None, :]
    pos_bias = jnp.stack([pos_bias, jnp.where(causal[None], pos_bias, -jnp.inf)])
    qcol, kcol = 0, dim // (2 * dh)
    return pl.pallas_call(
        functools.partial(_moba_kernel, nb=nb),
        grid=(batch, nh // 2, nb),
        in_specs=[pl.BlockSpec(memory_space=pltpu.SMEM),
                  pl.BlockSpec((bs, 2 * dh), lambda b, hp, i: (b * nb + i, qcol + hp)),
                  pl.BlockSpec((seq, 2 * dh), lambda b, hp, i: (b, kcol + hp)),
                  pl.BlockSpec((1, 2, nb, V_AUG_ROWS, bs), lambda b, hp, i: (b, hp, 0, 0, 0)),
                  pl.BlockSpec((2, 2, bs, bs), lambda b, hp, i: (0, hp, 0, 0))],
        out_specs=pl.BlockSpec((bs, 2 * dh), lambda b, hp, i: (b * nb + i, hp)),
        out_shape=jax.ShapeDtypeStruct((n, dim), BF16),
        scratch_shapes=[pltpu.VMEM((nb, 2 * dh), F32), pltpu.VMEM((2, nb, bs), F32),
                        pltpu.VMEM((4, bs, bs), F32)],
        compiler_params=_params("parallel", "parallel", "arbitrary"),
        name="moba",
    )(slopes, m3, m3, vt, pos_bias)


def _merge_kernel(x_ref, oa_ref, ob_ref, g_ref, wa_ref, wb_ref, wo_ref, o_ref):
    d = x_ref.shape[1]
    ga = jax.nn.sigmoid(g_ref[:, 0:d])
    gb = jax.nn.sigmoid(g_ref[:, d:2 * d])
    merged = ga * _dot(oa_ref[...], wa_ref[...]) + gb * _dot(ob_ref[...], wb_ref[...])
    o_ref[...] = x_ref[...] + _dot(merged.astype(BF16), wo_ref[...])


def _merge(x, oa, ob, gates, wa, wb, wo, tm=512):
    n, d = x.shape

    def row(width):
        return pl.BlockSpec((tm, width), lambda i: (i, 0))

    return pl.pallas_call(
        _merge_kernel,
        grid=(n // tm,),
        in_specs=[row(d), row(oa.shape[1]), row(ob.shape[1]), row(gates.shape[1]),
                  _resident(wa.shape), _resident(wb.shape), _resident(wo.shape)],
        out_specs=row(d),
        out_shape=jax.ShapeDtypeStruct((n, d), F32),
        compiler_params=_params("parallel"),
        name="merge",
    )(x, oa, ob, gates, wa, wb, wo)


def kernel(x, ffn1_norm, ffn1_w_gu, ffn1_w_down, mix_norm, w_in, hgrn_lb_logits, hgrn_norm,
           w_branch_a, w_branch_b, w_out, ffn2_norm, ffn2_w_gu, ffn2_w_down, final_norm):
    batch, seq, d = x.shape
    depth = ffn1_norm.shape[0]
    ff = ffn1_w_down.shape[1]
    hdim = 4 * HGRN_HEADS * HGRN_DK
    mdim = 3 * MOBA_HEADS * MOBA_HEAD_DIM
    xf = x.reshape(batch * seq, d)
    for l in range(depth):
        wgu = ffn1_w_gu[l].astype(BF16)
        xf = _ffn(xf, ffn1_norm[l], wgu[:, :ff], wgu[:, ff:], ffn1_w_down[l].astype(BF16))
        wi = w_in[l].astype(BF16)
        h4, m3, gates = _inproj(xf, mix_norm[l], wi[:, :hdim], wi[:, hdim:hdim + mdim], wi[:, hdim + mdim:])
        oa = _hgrn(h4, hgrn_lb_logits, hgrn_norm[l], l, batch, seq)
        ob = _moba(m3, batch, seq)
        xf = _merge(xf, oa, ob, gates, w_branch_a[l].astype(BF16), w_branch_b[l].astype(BF16),
                    w_out[l].astype(BF16))
        wgu = ffn2_w_gu[l].astype(BF16)
        xf = _ffn(xf, ffn2_norm[l], wgu[:, :ff], wgu[:, ff:], ffn2_w_down[l].astype(BF16),
                  final_w=final_norm if l == depth - 1 else None)
    return xf.reshape(batch, seq, d)
```

```python
import functools

import numpy as np
import jax
import jax.numpy as jnp
from jax import lax
from jax.experimental import pallas as pl
from jax.experimental.pallas import tpu as pltpu

F32 = jnp.float32
BF16 = jnp.bfloat16

NORM_EPS = 1e-6
MIN_FORGET = 1e-20
MASK_VALUE = -1e9

HGRN_HEADS = 4
HGRN_DK = 128
MOBA_HEADS = 8
MOBA_HEAD_DIM = 64
MOBA_BLOCK = 256
MOBA_TOPK = 3

V7X_VMEM_LIMIT_BYTES = 56 * 1024 * 1024

LOG2_E = 1.4426950408889634
HGRN_CHUNK = 128
HGRN_MATMUL_LEVEL_LIMIT = 8
HGRN_ROWS_PER_STEP = 512
V_AUG_ROWS = 80


def _rmsnorm(x, w):
    return x * lax.rsqrt(jnp.mean(x * x, axis=-1, keepdims=True) + NORM_EPS) * w


def _silu(x):
    return x * jax.nn.sigmoid(x)


def _dot(a, b):
    return jnp.dot(a, b, preferred_element_type=F32)


def _dot_nt(a, b):
    return lax.dot_general(a, b, (((1,), (1,)), ((), ())), preferred_element_type=F32)


def _resident(shape):
    nd = len(shape)
    return pl.BlockSpec(shape, lambda *_: (0,) * nd, pipeline_mode=pl.Buffered(1))


def _params(*sem):
    return pltpu.CompilerParams(dimension_semantics=sem, vmem_limit_bytes=V7X_VMEM_LIMIT_BYTES)


def _ffn_kernel(x_ref, nw_ref, wg_ref, wu_ref, wd_ref, *rest, final):
    o_ref = rest[-1]
    x = x_ref[...]
    h = _rmsnorm(x, nw_ref[...]).astype(BF16)
    g = _dot(h, wg_ref[...])
    u = _dot(h, wu_ref[...])
    a = (_silu(g) * u).astype(BF16)
    y = x + 0.5 * _dot(a, wd_ref[...])
    if final:
        y = _rmsnorm(y, rest[0][...])
    o_ref[...] = y


def _ffn(x, nw, wg, wu, wd, final_w=None, tm=256):
    n, d = x.shape
    ff = wg.shape[1]
    row = pl.BlockSpec((tm, d), lambda i: (i, 0))
    in_specs = [row, _resident((1, d)), _resident((d, ff)), _resident((d, ff)), _resident((ff, d))]
    args = [x, nw.reshape(1, d), wg, wu, wd]
    if final_w is not None:
        in_specs.append(_resident((1, d)))
        args.append(final_w.reshape(1, d))
    return pl.pallas_call(
        functools.partial(_ffn_kernel, final=final_w is not None),
        grid=(n // tm,),
        in_specs=in_specs,
        out_specs=row,
        out_shape=jax.ShapeDtypeStruct((n, d), F32),
        compiler_params=_params("parallel"),
        name="ffn",
    )(*args)


def _inproj_kernel(x_ref, nw_ref, wh_ref, wqk_ref, wvt_ref, wg_ref, h_ref, qk_ref, vt_ref, g_ref):
    h = _rmsnorm(x_ref[...], nw_ref[...]).astype(BF16)
    h_ref[...] = _dot(h, wh_ref[...])
    qk_ref[...] = _dot(h, wqk_ref[...]).astype(BF16)
    vt = _dot_nt(wvt_ref[...], h).astype(BF16)
    for blk in range(vt_ref.shape[0]):
        vt_ref[blk] = vt[:, blk * MOBA_BLOCK:(blk + 1) * MOBA_BLOCK]
    g_ref[...] = _dot(h, wg_ref[...]).astype(BF16)


def _inproj(x, nw, wh, wqk, wvt, wg, tm=256):
    n, d = x.shape
    dh, dqk, dv, dg = wh.shape[1], wqk.shape[1], wvt.shape[0], wg.shape[1]
    blocks = tm // MOBA_BLOCK
    return pl.pallas_call(
        _inproj_kernel,
        grid=(n // tm,),
        in_specs=[pl.BlockSpec((tm, d), lambda i: (i, 0)), _resident((1, d)),
                  _resident((d, dh)), _resident((d, dqk)), _resident((dv, d)), _resident((d, dg))],
        out_specs=[pl.BlockSpec((tm, dh), lambda i: (i, 0)),
                   pl.BlockSpec((tm, dqk), lambda i: (i, 0)),
                   pl.BlockSpec((blocks, dv, MOBA_BLOCK), lambda i: (i, 0, 0)),
                   pl.BlockSpec((tm, dg), lambda i: (i, 0))],
        out_shape=[jax.ShapeDtypeStruct((n, dh), F32),
                   jax.ShapeDtypeStruct((n, dqk), BF16),
                   jax.ShapeDtypeStruct((n // MOBA_BLOCK, dv, MOBA_BLOCK), BF16),
                   jax.ShapeDtypeStruct((n, dg), BF16)],
        compiler_params=_params("parallel"),
        name="inproj",
    )(x, nw.reshape(1, d), wh, wqk, wvt, wg)


def _hgrn_constants(c):
    t = np.arange(c)
    sp = t[None, :]
    tt = t[:, None]
    rows = [sp <= tt, sp > tt]
    masks, upper = [], []
    m = 1
    while m < c:
        up = (t // m) % 2 == 1
        bnd = ((t // (2 * m)) * 2 * m + m - 1)[:, None]
        if m < HGRN_MATMUL_LEVEL_LIMIT:
            rows.append(np.where(up[:, None], (sp > bnd) & (sp <= tt), (sp > tt) & (sp <= bnd)))
        same = (tt // (2 * m)) == (sp // (2 * m))
        masks.append(up[:, None] & (~up)[None, :] & same)
        upper.append(np.broadcast_to(up[:, None], (c, HGRN_DK)))
        m *= 2
    w = np.concatenate(rows, axis=0).astype(np.float32)
    return (jnp.asarray(w, BF16), jnp.asarray(np.stack(masks), F32),
            jnp.asarray(np.stack(upper), F32))


def _hgrn_kernel(lbl_ref, nw_ref, w_ref, lm_ref, up_ref, h_ref, o_ref, st_ref, *, layer, n_levels):
    c = HGRN_CHUNK
    dk = HGRN_DK
    kdim = HGRN_HEADS * dk

    @pl.when(pl.program_id(1) == 0)
    def _():
        st_ref[...] = jnp.zeros_like(st_ref)

    logits = lbl_ref[...]
    e = jnp.exp(logits - jnp.max(logits, axis=0, keepdims=True))
    sm = e / jnp.sum(e, axis=0, keepdims=True)
    cum = sm[0:1, :]
    for j in range(1, layer + 1):
        cum = cum + sm[j:j + 1, :]
    lb_all = jnp.clip(cum - sm[0:1, :], 0.0, 1.0)
    w = w_ref[...]
    nw = nw_ref[...]

    heads = [slice(h * dk, (h + 1) * dk) for h in range(HGRN_HEADS)]

    def chunk(ci, carry):
        r0 = pl.multiple_of(ci * c, c)
        q = _silu(h_ref[pl.ds(r0, c), 0:kdim])
        sig = jax.nn.sigmoid(h_ref[pl.ds(r0, c), kdim:2 * kdim])
        inp = h_ref[pl.ds(r0, c), 2 * kdim:3 * kdim]
        og = h_ref[pl.ds(r0, c), 3 * kdim:4 * kdim]
        f = lb_all + (1.0 - lb_all) * sig
        g = jnp.log(jnp.maximum(f, MIN_FORGET)) * LOG2_E
        k = (1.0 - lb_all) * (1.0 - sig)

        g_hi = g.astype(BF16)
        g_lo = (g - g_hi.astype(F32)).astype(BF16)
        r = _dot(w, jnp.concatenate([g_hi, g_lo], axis=1))
        e = r[:, :kdim] + r[:, kdim:]
        x = jnp.exp2(e)
        b = e[0:c]

        qd = (q * x[0:c]).astype(BF16)
        kd = (k * x[c:2 * c]).astype(BF16)
        decay_all = x[c - 1:c, :]
        dqk = q - k
        a = [jnp.zeros((c, c), F32) for _ in heads]
        for l in range(n_levels):
            m = 2 ** l
            if m < HGRN_MATMUL_LEVEL_LIMIT:
                xl = x[(2 + l) * c:(3 + l) * c]
            else:
                b3 = b.reshape(c // (2 * m), 2 * m, kdim)
                xl = jnp.exp2(-jnp.abs(b3 - b3[:, m - 1:m, :])).reshape(c, kdim)
            up = jnp.concatenate([up_ref[l]] * HGRN_HEADS, axis=1)
            z = ((k + up * dqk) * xl).astype(BF16)
            mask = lm_ref[l]
            a = [a[h] + mask * _dot_nt(z[:, hs], z[:, hs]) for h, hs in enumerate(heads)]
        inp_b = inp.astype(BF16)
        qk = q * k
        outs = []
        for h, hs in enumerate(heads):
            st = st_ref[h]
            o = (_dot(a[h].astype(BF16), inp_b[:, hs]) + _dot_nt(qd[:, hs], st.astype(BF16))
                 + jnp.sum(qk[:, hs], axis=-1, keepdims=True) * inp[:, hs])
            st_ref[h] = st * decay_all[:, hs] + _dot(inp[:, hs].T.astype(BF16), kd[:, hs])
            outs.append(o * lax.rsqrt(jnp.mean(o * o, axis=-1, keepdims=True) + NORM_EPS) * nw)
        o_ref[pl.ds(r0, c), :] = (jnp.concatenate(outs, axis=1) * _silu(og)).astype(o_ref.dtype)
        return carry

    lax.fori_loop(0, h_ref.shape[0] // c, chunk, 0)


def _hgrn(h4, lb_logits, norm_w, layer, batch, seq):
    n = h4.shape[0]
    c = HGRN_CHUNK
    tb = HGRN_ROWS_PER_STEP
    steps = seq // tb
    w, lm, up = _hgrn_constants(c)
    n_levels = lm.shape[0]
    vdim = HGRN_HEADS * HGRN_DK
    return pl.pallas_call(
        functools.partial(_hgrn_kernel, layer=layer, n_levels=n_levels),
        grid=(batch, steps),
        in_specs=[_resident(lb_logits.shape), _resident((1, HGRN_DK)), _resident(w.shape),
                  _resident(lm.shape), _resident(up.shape),
                  pl.BlockSpec((tb, h4.shape[1]), lambda b, s: (b * steps + s, 0))],
        out_specs=pl.BlockSpec((tb, vdim), lambda b, s: (b * steps + s, 0)),
        out_shape=jax.ShapeDtypeStruct((n, vdim), BF16),
        scratch_shapes=[pltpu.VMEM((HGRN_HEADS, HGRN_DK, HGRN_DK), F32)],
        compiler_params=_params("parallel", "arbitrary"),
        name="hgrn",
    )(lb_logits, norm_w.reshape(1, HGRN_DK), w, lm, up, h4)


def _moba_kernel(slope_ref, q_ref, k_ref, vt_ref, pb_ref, o_ref, km_ref, sel_ref, s_ref, *, nb):
    bs = MOBA_BLOCK
    dh = MOBA_HEAD_DIM
    hp = pl.program_id(1)
    i = pl.program_id(2)

    @pl.when(i == 0)
    def _():
        for j in range(nb):
            km_ref[j:j + 1, :] = jnp.mean(k_ref[j * bs:(j + 1) * bs, :].astype(F32), axis=0, keepdims=True)

    q = q_ref[...]
    lane = lax.broadcasted_iota(jnp.int32, (1, 2 * dh), 1)
    head_lanes = [lane < dh, lane >= dh]

    km = km_ref[...]
    km2 = jnp.concatenate([jnp.where(head_lanes[0], km, 0.0), jnp.where(head_lanes[1], km, 0.0)], axis=0)
    p0 = km2.astype(BF16)
    r1 = km2 - p0.astype(F32)
    p1 = r1.astype(BF16)
    p2 = (r1 - p1.astype(F32)).astype(BF16)
    gate2 = _dot_nt(p0, q) + _dot_nt(p1, q) + _dot_nt(p2, q)

    jidx = lax.broadcasted_iota(jnp.int32, (nb, bs), 0)
    past = jidx < i
    kt = min(MOBA_TOPK, nb)
    for a in range(2):
        gm = jnp.where(past, gate2[a * nb:(a + 1) * nb, :], MASK_VALUE)
        cnt = jnp.zeros((nb, bs), F32)
        for jp in range(nb):
            row = gm[jp:jp + 1, :]
            beats = (row > gm) | ((row == gm) & (jp < jidx))
            cnt = cnt + jnp.where(beats, 1.0, 0.0) * (jp < i).astype(F32)
        sel_ref[a] = jnp.where((past & (cnt < kt)) | (jidx == i), 1.0, 0.0)

    scale = dh ** -0.5
    qm = [jnp.where(head_lanes[a], q, jnp.zeros_like(q)) * jnp.asarray(scale, BF16) for a in range(2)]
    slopes = [slope_ref[2 * hp + a] for a in range(2)]

    def block_ids(t):
        ids = []
        for u in range(2):
            b = i - 2 * t - u
            ids.append((jnp.clip(b, 0, i), (b >= 0).astype(F32)))
        return ids

    def scores(t):
        s_all, stats = [], []
        for u, (bc, valid) in enumerate(block_ids(t)):
            k_b = k_ref[pl.ds(pl.multiple_of(bc * bs, bs), bs), :]
            diag = (bc == i).astype(jnp.int32)
            for a in range(2):
                s = _dot_nt(k_b, qm[a]) + pb_ref[diag, a]
                chosen = sel_ref[a, pl.ds(bc, 1), :] * valid > 0.5
                off = jnp.where(chosen, slopes[a] * (bc.astype(F32) * float(bs)), -jnp.inf)
                s_all.append(s)
                stats.append((jnp.max(s, axis=0, keepdims=True) + off, off))
        return s_all, tuple(stats)

    ones_rows = jnp.ones((V_AUG_ROWS - dh, 2 * bs), BF16)

    def step(t, carry):
        stats, state = carry
        s_cur = [s_ref[ua] for ua in range(4)]
        s_nxt, stats_nxt = scores(t + 1)
        ids = block_ids(t)
        out = []
        for a in range(2):
            m_old, acc = state[a]
            m_new = jnp.maximum(m_old, jnp.maximum(stats[a][0], stats[2 + a][0]))
            p = [jnp.exp((s_cur[2 * u + a] - (m_new - stats[2 * u + a][1])).astype(BF16)) for u in range(2)]
            v = jnp.concatenate([vt_ref[ids[u][0], a * dh:(a + 1) * dh, :] for u in range(2)], axis=1)
            pv = _dot(jnp.concatenate([v, ones_rows], axis=0), jnp.concatenate(p, axis=0))
            out.append((m_new, jnp.exp(m_old - m_new) * acc + pv))
        for ua in range(4):
            s_ref[ua] = s_nxt[ua]
        return stats_nxt, tuple(out)

    s0, stats0 = scores(0)
    for ua in range(4):
        s_ref[ua] = s0[ua]
    init = tuple((jnp.full((1, bs), -jnp.inf, F32), jnp.zeros((V_AUG_ROWS, bs), F32)) for _ in range(2))
    _, state = lax.fori_loop(0, (i + 2) // 2, step, (stats0, init))

    outs = [state[a][1][0:dh] / state[a][1][dh:dh + 1] for a in range(2)]
    o_ref[...] = jnp.concatenate(outs, axis=0).T.astype(o_ref.dtype)


def _moba(qk, vt, batch, seq):
    n = qk.shape[0]
    bs, dh, nh = MOBA_BLOCK, MOBA_HEAD_DIM, MOBA_HEADS
    nb = seq // bs
    dim = nh * dh
    slopes = jnp.exp2(-8.0 * jnp.arange(1, nh + 1, dtype=F32) / nh)
    key_pos = jnp.arange(bs, dtype=F32)[:, None]
    pos_bias = slopes[:, None, None] * jnp.broadcast_to(key_pos, (bs, bs))[None]
    causal = jnp.arange(bs)[:, None] <= jnp.arange(bs)[None, :]
    pos_bias = jnp.stack([pos_bias, jnp.where(causal[None], pos_bias, -jnp.inf)])
    qcol, kcol = 0, dim // (2 * dh)
    return pl.pallas_call(
        functools.partial(_moba_kernel, nb=nb),
        grid=(batch, nh // 2, nb),
        in_specs=[pl.BlockSpec(memory_space=pltpu.SMEM),
                  pl.BlockSpec((bs, 2 * dh), lambda b, hp, i: (b * nb + i, qcol + hp)),
                  pl.BlockSpec((seq, 2 * dh), lambda b, hp, i: (b, kcol + hp)),
                  pl.BlockSpec((nb, 2 * dh, bs), lambda b, hp, i: (b, hp, 0)),
                  pl.BlockSpec((2, 2, bs, bs), lambda b, hp, i: (0, hp, 0, 0))],
        out_specs=pl.BlockSpec((bs, 2 * dh), lambda b, hp, i: (b * nb + i, hp)),
        out_shape=jax.ShapeDtypeStruct((n, dim), BF16),
        scratch_shapes=[pltpu.VMEM((nb, 2 * dh), F32), pltpu.VMEM((2, nb, bs), F32),
                        pltpu.VMEM((4, bs, bs), F32)],
        compiler_params=_params("parallel", "parallel", "arbitrary"),
        name="moba",
    )(slopes, qk, qk, vt, pos_bias)


def _merge_kernel(x_ref, oa_ref, ob_ref, g_ref, wa_ref, wb_ref, wo_ref, o_ref):
    d = x_ref.shape[1]
    ga = jax.nn.sigmoid(g_ref[:, 0:d].astype(F32))
    gb = jax.nn.sigmoid(g_ref[:, d:2 * d].astype(F32))
    merged = ga * _dot(oa_ref[...], wa_ref[...]) + gb * _dot(ob_ref[...], wb_ref[...])
    o_ref[...] = x_ref[...] + _dot(merged.astype(BF16), wo_ref[...])


def _merge(x, oa, ob, gates, wa, wb, wo, tm=512):
    n, d = x.shape

    def row(width):
        return pl.BlockSpec((tm, width), lambda i: (i, 0))

    return pl.pallas_call(
        _merge_kernel,
        grid=(n // tm,),
        in_specs=[row(d), row(oa.shape[1]), row(ob.shape[1]), row(gates.shape[1]),
                  _resident(wa.shape), _resident(wb.shape), _resident(wo.shape)],
        out_specs=row(d),
        out_shape=jax.ShapeDtypeStruct((n, d), F32),
        compiler_params=_params("parallel"),
        name="merge",
    )(x, oa, ob, gates, wa, wb, wo)


def kernel(x, ffn1_norm, ffn1_w_gu, ffn1_w_down, mix_norm, w_in, hgrn_lb_logits, hgrn_norm,
           w_branch_a, w_branch_b, w_out, ffn2_norm, ffn2_w_gu, ffn2_w_down, final_norm):
    batch, seq, d = x.shape
    depth = ffn1_norm.shape[0]
    ff = ffn1_w_down.shape[1]
    hdim = 4 * HGRN_HEADS * HGRN_DK
    mdim = MOBA_HEADS * MOBA_HEAD_DIM
    xf = x.reshape(batch * seq, d)
    for l in range(depth):
        wgu = ffn1_w_gu[l].astype(BF16)
        xf = _ffn(xf, ffn1_norm[l], wgu[:, :ff], wgu[:, ff:], ffn1_w_down[l].astype(BF16))
        wi = w_in[l].astype(BF16)
        h4, qk, vt, gates = _inproj(xf, mix_norm[l], wi[:, :hdim], wi[:, hdim:hdim + 2 * mdim],
                                    wi[:, hdim + 2 * mdim:hdim + 3 * mdim].T, wi[:, hdim + 3 * mdim:])
        oa = _hgrn(h4, hgrn_lb_logits, hgrn_norm[l], l, batch, seq)
        ob = _moba(qk, vt, batch, seq)
        xf = _merge(xf, oa, ob, gates, w_branch_a[l].astype(BF16), w_branch_b[l].astype(BF16),
                    w_out[l].astype(BF16))
        wgu = ffn2_w_gu[l].astype(BF16)
        xf = _ffn(xf, ffn2_norm[l], wgu[:, :ff], wgu[:, ff:], ffn2_w_down[l].astype(BF16),
                  final_w=final_norm if l == depth - 1 else None)
    return xf.reshape(batch, seq, d)
```

```python
import functools

import numpy as np
import jax
import jax.numpy as jnp
from jax import lax
from jax.experimental import pallas as pl
from jax.experimental.pallas import tpu as pltpu

F32 = jnp.float32
BF16 = jnp.bfloat16

NORM_EPS = 1e-6
MIN_FORGET = 1e-20
MASK_VALUE = -1e9

HGRN_HEADS = 4
HGRN_DK = 128
MOBA_HEADS = 8
MOBA_HEAD_DIM = 64
MOBA_BLOCK = 256
MOBA_TOPK = 3

V7X_VMEM_LIMIT_BYTES = 56 * 1024 * 1024

LOG2_E = 1.4426950408889634
HGRN_CHUNK = 128
HGRN_MATMUL_LEVEL_LIMIT = 8
HGRN_ROWS_PER_STEP = 512
V_AUG_ROWS = 80


def _rmsnorm(x, w):
    return x * lax.rsqrt(jnp.mean(x * x, axis=-1, keepdims=True) + NORM_EPS) * w


def _silu(x):
    return x * jax.nn.sigmoid(x)


def _dot(a, b):
    return jnp.dot(a, b, preferred_element_type=F32)


def _dot_nt(a, b):
    return lax.dot_general(a, b, (((1,), (1,)), ((), ())), preferred_element_type=F32)


def _resident(shape):
    nd = len(shape)
    return pl.BlockSpec(shape, lambda *_: (0,) * nd, pipeline_mode=pl.Buffered(1))


def _params(*sem):
    return pltpu.CompilerParams(dimension_semantics=sem, vmem_limit_bytes=V7X_VMEM_LIMIT_BYTES)


def _ffn_kernel(x_ref, *refs, merge, final):
    refs = list(refs)
    o_ref = refs.pop()
    x = x_ref[...]
    if merge:
        oa_ref, ob_ref, g_ref, wa_ref, wb_ref, wo_ref = refs[:6]
        refs = refs[6:]
        d = x.shape[1]
        ga = jax.nn.sigmoid(g_ref[:, 0:d].astype(F32))
        gb = jax.nn.sigmoid(g_ref[:, d:2 * d].astype(F32))
        merged = ga * _dot(oa_ref[...], wa_ref[...]) + gb * _dot(ob_ref[...], wb_ref[...])
        x = x + _dot(merged.astype(BF16), wo_ref[...])
    nw_ref, wg_ref, wu_ref, wd_ref = refs[:4]
    h = _rmsnorm(x, nw_ref[...]).astype(BF16)
    g = _dot(h, wg_ref[...])
    u = _dot(h, wu_ref[...])
    a = (_silu(g) * u).astype(BF16)
    y = x + 0.5 * _dot(a, wd_ref[...])
    if final:
        y = _rmsnorm(y, refs[4][...])
    o_ref[...] = y


def _ffn(x, nw, wg, wu, wd, merge=None, final_w=None, tm=512):
    n, d = x.shape
    ff = wg.shape[1]

    def row(width):
        return pl.BlockSpec((tm, width), lambda i: (i, 0))

    in_specs, args = [row(d)], [x]
    if merge is not None:
        in_specs += [row(a.shape[1]) for a in merge[:3]] + [_resident(a.shape) for a in merge[3:]]
        args += list(merge)
    in_specs += [_resident((1, d)), _resident((d, ff)), _resident((d, ff)), _resident((ff, d))]
    args += [nw.reshape(1, d), wg, wu, wd]
    if final_w is not None:
        in_specs.append(_resident((1, d)))
        args.append(final_w.reshape(1, d))
    return pl.pallas_call(
        functools.partial(_ffn_kernel, merge=merge is not None, final=final_w is not None),
        grid=(n // tm,),
        in_specs=in_specs,
        out_specs=row(d),
        out_shape=jax.ShapeDtypeStruct((n, d), F32),
        compiler_params=_params("parallel"),
        name="ffn",
    )(*args)


def _inproj_kernel(x_ref, nw_ref, wh_ref, wqk_ref, wvt_ref, wg_ref, h_ref, qk_ref, vt_ref, g_ref):
    h = _rmsnorm(x_ref[...], nw_ref[...]).astype(BF16)
    h_ref[...] = _dot(h, wh_ref[...])
    qk_ref[...] = _dot(h, wqk_ref[...]).astype(BF16)
    vt = _dot_nt(wvt_ref[...], h).astype(BF16)
    for blk in range(vt_ref.shape[0]):
        vt_ref[blk] = vt[:, blk * MOBA_BLOCK:(blk + 1) * MOBA_BLOCK]
    g_ref[...] = _dot(h, wg_ref[...]).astype(BF16)


def _inproj(x, nw, wh, wqk, wvt, wg, tm=512):
    n, d = x.shape
    dh, dqk, dv, dg = wh.shape[1], wqk.shape[1], wvt.shape[0], wg.shape[1]
    blocks = tm // MOBA_BLOCK
    return pl.pallas_call(
        _inproj_kernel,
        grid=(n // tm,),
        in_specs=[pl.BlockSpec((tm, d), lambda i: (i, 0)), _resident((1, d)),
                  _resident((d, dh)), _resident((d, dqk)), _resident((dv, d)), _resident((d, dg))],
        out_specs=[pl.BlockSpec((tm, dh), lambda i: (i, 0)),
                   pl.BlockSpec((tm, dqk), lambda i: (i, 0)),
                   pl.BlockSpec((blocks, dv, MOBA_BLOCK), lambda i: (i, 0, 0)),
                   pl.BlockSpec((tm, dg), lambda i: (i, 0))],
        out_shape=[jax.ShapeDtypeStruct((n, dh), F32),
                   jax.ShapeDtypeStruct((n, dqk), BF16),
                   jax.ShapeDtypeStruct((n // MOBA_BLOCK, dv, MOBA_BLOCK), BF16),
                   jax.ShapeDtypeStruct((n, dg), BF16)],
        compiler_params=_params("parallel"),
        name="inproj",
    )(x, nw.reshape(1, d), wh, wqk, wvt, wg)


def _hgrn_constants(c):
    t = np.arange(c)
    sp = t[None, :]
    tt = t[:, None]
    rows = [sp <= tt, sp > tt]
    masks, upper = [], []
    m = 1
    while m < c:
        up = (t // m) % 2 == 1
        bnd = ((t // (2 * m)) * 2 * m + m - 1)[:, None]
        if m < HGRN_MATMUL_LEVEL_LIMIT:
            rows.append(np.where(up[:, None], (sp > bnd) & (sp <= tt), (sp > tt) & (sp <= bnd)))
        same = (tt // (2 * m)) == (sp // (2 * m))
        masks.append(up[:, None] & (~up)[None, :] & same)
        upper.append(np.broadcast_to(up[:, None], (c, HGRN_DK)))
        m *= 2
    w = np.concatenate(rows, axis=0).astype(np.float32)
    return (jnp.asarray(w, BF16), jnp.asarray(np.stack(masks), F32),
            jnp.asarray(np.stack(upper), F32))


def _hgrn_kernel(lbl_ref, nw_ref, w_ref, lm_ref, up_ref, h_ref, o_ref, st_ref, *, layer, n_levels):
    c = HGRN_CHUNK
    dk = HGRN_DK
    kdim = HGRN_HEADS * dk

    @pl.when(pl.program_id(1) == 0)
    def _():
        st_ref[...] = jnp.zeros_like(st_ref)

    logits = lbl_ref[...]
    e = jnp.exp(logits - jnp.max(logits, axis=0, keepdims=True))
    sm = e / jnp.sum(e, axis=0, keepdims=True)
    cum = sm[0:1, :]
    for j in range(1, layer + 1):
        cum = cum + sm[j:j + 1, :]
    lb_all = jnp.clip(cum - sm[0:1, :], 0.0, 1.0)
    w = w_ref[...]
    nw = nw_ref[...]

    heads = [slice(h * dk, (h + 1) * dk) for h in range(HGRN_HEADS)]

    def chunk(ci, carry):
        r0 = pl.multiple_of(ci * c, c)
        q = _silu(h_ref[pl.ds(r0, c), 0:kdim])
        sig = jax.nn.sigmoid(h_ref[pl.ds(r0, c), kdim:2 * kdim])
        inp = h_ref[pl.ds(r0, c), 2 * kdim:3 * kdim]
        og = h_ref[pl.ds(r0, c), 3 * kdim:4 * kdim]
        f = lb_all + (1.0 - lb_all) * sig
        g = jnp.log(jnp.maximum(f, MIN_FORGET)) * LOG2_E
        k = (1.0 - lb_all) * (1.0 - sig)

        g_hi = g.astype(BF16)
        g_lo = (g - g_hi.astype(F32)).astype(BF16)
        r = _dot(w, jnp.concatenate([g_hi, g_lo], axis=1))
        e = r[:, :kdim] + r[:, kdim:]
        x = jnp.exp2(e)
        b = e[0:c]

        qd = (q * x[0:c]).astype(BF16)
        kd = (k * x[c:2 * c]).astype(BF16)
        decay_all = x[c - 1:c, :]
        dqk = q - k
        a = [jnp.zeros((c, c), F32) for _ in heads]
        for l in range(n_levels):
            m = 2 ** l
            if m < HGRN_MATMUL_LEVEL_LIMIT:
                xl = x[(2 + l) * c:(3 + l) * c]
            else:
                b3 = b.reshape(c // (2 * m), 2 * m, kdim)
                xl = jnp.exp2(-jnp.abs(b3 - b3[:, m - 1:m, :])).reshape(c, kdim)
            up = jnp.concatenate([up_ref[l]] * HGRN_HEADS, axis=1)
            z = ((k + up * dqk) * xl).astype(BF16)
            mask = lm_ref[l]
            a = [a[h] + mask * _dot_nt(z[:, hs], z[:, hs]) for h, hs in enumerate(heads)]
        inp_b = inp.astype(BF16)
        qk = q * k
        outs = []
        for h, hs in enumerate(heads):
            st = st_ref[h]
            o = (_dot(a[h].astype(BF16), inp_b[:, hs]) + _dot_nt(qd[:, hs], st.astype(BF16))
                 + jnp.sum(qk[:, hs], axis=-1, keepdims=True) * inp[:, hs])
            st_ref[h] = st * decay_all[:, hs] + _dot(inp[:, hs].T.astype(BF16), kd[:, hs])
            outs.append(o * lax.rsqrt(jnp.mean(o * o, axis=-1, keepdims=True) + NORM_EPS) * nw)
        o_ref[pl.ds(r0, c), :] = (jnp.concatenate(outs, axis=1) * _silu(og)).astype(o_ref.dtype)
        return carry

    lax.fori_loop(0, h_ref.shape[0] // c, chunk, 0)


def _hgrn(h4, lb_logits, norm_w, layer, batch, seq):
    n = h4.shape[0]
    c = HGRN_CHUNK
    tb = HGRN_ROWS_PER_STEP
    steps = seq // tb
    w, lm, up = _hgrn_constants(c)
    n_levels = lm.shape[0]
    vdim = HGRN_HEADS * HGRN_DK
    return pl.pallas_call(
        functools.partial(_hgrn_kernel, layer=layer, n_levels=n_levels),
        grid=(batch, steps),
        in_specs=[_resident(lb_logits.shape), _resident((1, HGRN_DK)), _resident(w.shape),
                  _resident(lm.shape), _resident(up.shape),
                  pl.BlockSpec((tb, h4.shape[1]), lambda b, s: (b * steps + s, 0))],
        out_specs=pl.BlockSpec((tb, vdim), lambda b, s: (b * steps + s, 0)),
        out_shape=jax.ShapeDtypeStruct((n, vdim), BF16),
        scratch_shapes=[pltpu.VMEM((HGRN_HEADS, HGRN_DK, HGRN_DK), F32)],
        compiler_params=_params("parallel", "arbitrary"),
        name="hgrn",
    )(lb_logits, norm_w.reshape(1, HGRN_DK), w, lm, up, h4)


def _moba_kernel(slope_ref, tab_ref, q_ref, k_ref, vt_ref, pb_ref, o_ref, km_ref, sel_ref, s_ref, *, nb, n_steps):
    bs = MOBA_BLOCK
    dh = MOBA_HEAD_DIM
    hp = pl.program_id(1)

    for j in range(nb):
        km_ref[j:j + 1, :] = jnp.mean(k_ref[j * bs:(j + 1) * bs, :].astype(F32), axis=0, keepdims=True)

    lane = lax.broadcasted_iota(jnp.int32, (1, 2 * dh), 1)
    head_lanes = [lane < dh, lane >= dh]

    km = km_ref[...]
    km2 = jnp.concatenate([jnp.where(head_lanes[0], km, 0.0), jnp.where(head_lanes[1], km, 0.0)], axis=0)
    p0 = km2.astype(BF16)
    r1 = km2 - p0.astype(F32)
    p1 = r1.astype(BF16)
    p2 = (r1 - p1.astype(F32)).astype(BF16)
    jidx = lax.broadcasted_iota(jnp.int32, (nb, bs), 0)
    kt = min(MOBA_TOPK, nb)

    def query_block(i):
        return q_ref[pl.ds(pl.multiple_of(i * bs, bs), bs), :]

    def select(i, carry):
        q = query_block(i)
        gate2 = _dot_nt(p0, q) + _dot_nt(p1, q) + _dot_nt(p2, q)
        past = jidx < i
        for a in range(2):
            gm = jnp.where(past, gate2[a * nb:(a + 1) * nb, :], MASK_VALUE)
            cnt = jnp.zeros((nb, bs), F32)
            for jp in range(nb):
                row = gm[jp:jp + 1, :]
                beats = (row > gm) | ((row == gm) & (jp < jidx))
                cnt = cnt + jnp.where(beats, 1.0, 0.0) * (jp < i).astype(F32)
            sel_ref[i, a] = jnp.where((past & (cnt < kt)) | (jidx == i), 1.0, 0.0)
        return carry

    lax.fori_loop(0, nb, select, 0)

    scale = dh ** -0.5
    slopes = [slope_ref[2 * hp + a] for a in range(2)]

    def block_ids(i, t):
        ids = []
        for u in range(2):
            b = i - 2 * t - u
            ids.append((jnp.maximum(b, 0), (b >= 0).astype(F32)))
        return ids

    def scores(i, t):
        q = query_block(i)
        qm = [jnp.where(head_lanes[a], q, jnp.zeros_like(q)) * jnp.asarray(scale, BF16) for a in range(2)]
        s_all, stats = [], []
        for u, (bc, valid) in enumerate(block_ids(i, t)):
            k_b = k_ref[pl.ds(pl.multiple_of(bc * bs, bs), bs), :]
            diag = (bc == i).astype(jnp.int32)
            for a in range(2):
                s = _dot_nt(k_b, qm[a]) + pb_ref[diag, a]
                chosen = sel_ref[i, a, pl.ds(bc, 1), :] * valid > 0.5
                off = jnp.where(chosen, slopes[a] * (bc.astype(F32) * float(bs)), -jnp.inf)
                s_all.append(s)
                stats.append((jnp.max(s, axis=0, keepdims=True) + off, off))
        return s_all, tuple(stats)

    ones_rows = jnp.ones((V_AUG_ROWS - dh, 2 * bs), BF16)

    def step(n, carry):
        stats, state = carry
        i, t = tab_ref[0, n], tab_ref[1, n]
        s_cur = [s_ref[ua] for ua in range(4)]
        s_nxt, stats_nxt = scores(tab_ref[0, n + 1], tab_ref[1, n + 1])
        ids = block_ids(i, t)
        first = t == 0
        cap = jnp.where(first, -jnp.inf, jnp.inf)
        keep = jnp.where(first, 0.0, 1.0)
        out = []
        for a in range(2):
            m_old = jnp.minimum(state[a][0], cap)
            m_new = jnp.maximum(m_old, jnp.maximum(stats[a][0], stats[2 + a][0]))
            p = [jnp.exp((s_cur[2 * u + a] - (m_new - stats[2 * u + a][1])).astype(BF16)) for u in range(2)]
            v = jnp.concatenate([vt_ref[ids[u][0], a * dh:(a + 1) * dh, :] for u in range(2)], axis=1)
            pv = _dot(jnp.concatenate([v, ones_rows], axis=0), jnp.concatenate(p, axis=0))
            out.append((m_new, (jnp.exp(m_old - m_new) * keep) * state[a][1] + pv))
        for ua in range(4):
            s_ref[ua] = s_nxt[ua]

        @pl.when(t == (i + 2) // 2 - 1)
        def _():
            o = [out[a][1][0:dh] / out[a][1][dh:dh + 1] for a in range(2)]
            o_ref[pl.ds(pl.multiple_of(i * bs, bs), bs), :] = jnp.concatenate(o, axis=0).T.astype(o_ref.dtype)

        return stats_nxt, tuple(out)

    s0, stats0 = scores(tab_ref[0, 0], tab_ref[1, 0])
    for ua in range(4):
        s_ref[ua] = s0[ua]
    init = tuple((jnp.full((1, bs), -jnp.inf, F32), jnp.zeros((V_AUG_ROWS, bs), F32)) for _ in range(2))
    lax.fori_loop(0, n_steps, step, (stats0, init))


def _moba(qk, vt, batch, seq):
    n = qk.shape[0]
    bs, dh, nh = MOBA_BLOCK, MOBA_HEAD_DIM, MOBA_HEADS
    nb = seq // bs
    dim = nh * dh
    slopes = jnp.exp2(-8.0 * jnp.arange(1, nh + 1, dtype=F32) / nh)
    key_pos = jnp.arange(bs, dtype=F32)[:, None]
    pos_bias = slopes[:, None, None] * jnp.broadcast_to(key_pos, (bs, bs))[None]
    causal = jnp.arange(bs)[:, None] <= jnp.arange(bs)[None, :]
    pos_bias = jnp.stack([pos_bias, jnp.where(causal[None], pos_bias, -jnp.inf)])
    table = [(i, t) for i in range(nb) for t in range((i + 2) // 2)] + [(0, 0)]
    table = jnp.asarray(np.array(table, np.int32).T)
    qcol, kcol = 0, dim // (2 * dh)
    smem = pl.BlockSpec(memory_space=pltpu.SMEM)
    return pl.pallas_call(
        functools.partial(_moba_kernel, nb=nb, n_steps=table.shape[1] - 1),
        grid=(batch, nh // 2),
        in_specs=[smem, smem,
                  pl.BlockSpec((seq, 2 * dh), lambda b, hp: (b, qcol + hp)),
                  pl.BlockSpec((seq, 2 * dh), lambda b, hp: (b, kcol + hp)),
                  pl.BlockSpec((nb, 2 * dh, bs), lambda b, hp: (b, hp, 0)),
                  pl.BlockSpec((2, 2, bs, bs), lambda b, hp: (0, hp, 0, 0))],
        out_specs=pl.BlockSpec((seq, 2 * dh), lambda b, hp: (b, hp)),
        out_shape=jax.ShapeDtypeStruct((n, dim), BF16),
        scratch_shapes=[pltpu.VMEM((nb, 2 * dh), F32), pltpu.VMEM((nb, 2, nb, bs), F32),
                        pltpu.VMEM((4, bs, bs), F32)],
        compiler_params=_params("parallel", "parallel"),
        name="moba",
    )(slopes, table, qk, qk, vt, pos_bias)


def kernel(x, ffn1_norm, ffn1_w_gu, ffn1_w_down, mix_norm, w_in, hgrn_lb_logits, hgrn_norm,
           w_branch_a, w_branch_b, w_out, ffn2_norm, ffn2_w_gu, ffn2_w_down, final_norm):
    batch, seq, d = x.shape
    depth = ffn1_norm.shape[0]
    ff = ffn1_w_down.shape[1]
    hdim = 4 * HGRN_HEADS * HGRN_DK
    mdim = MOBA_HEADS * MOBA_HEAD_DIM
    xf = x.reshape(batch * seq, d)
    for l in range(depth):
        wgu = ffn1_w_gu[l].astype(BF16)
        xf = _ffn(xf, ffn1_norm[l], wgu[:, :ff], wgu[:, ff:], ffn1_w_down[l].astype(BF16))
        wi = w_in[l].astype(BF16)
        h4, qk, vt, gates = _inproj(xf, mix_norm[l], wi[:, :hdim], wi[:, hdim:hdim + 2 * mdim],
                                    wi[:, hdim + 2 * mdim:hdim + 3 * mdim].T, wi[:, hdim + 3 * mdim:])
        oa = _hgrn(h4, hgrn_lb_logits, hgrn_norm[l], l, batch, seq)
        ob = _moba(qk, vt, batch, seq)
        merge = (oa, ob, gates, w_branch_a[l].astype(BF16), w_branch_b[l].astype(BF16), w_out[l].astype(BF16))
        wgu = ffn2_w_gu[l].astype(BF16)
        xf = _ffn(xf, ffn2_norm[l], wgu[:, :ff], wgu[:, ff:], ffn2_w_down[l].astype(BF16),
                  merge=merge, final_w=final_norm if l == depth - 1 else None)
    return xf.reshape(batch, seq, d)
```

```python
import functools

import numpy as np
import jax
import jax.numpy as jnp
from jax import lax
from jax.experimental import pallas as pl
from jax.experimental.pallas import tpu as pltpu

F32 = jnp.float32
BF16 = jnp.bfloat16

NORM_EPS = 1e-6
MIN_FORGET = 1e-20
MASK_VALUE = -1e9

HGRN_HEADS = 4
HGRN_DK = 128
MOBA_HEADS = 8
MOBA_HEAD_DIM = 64
MOBA_BLOCK = 256
MOBA_TOPK = 3

V7X_VMEM_LIMIT_BYTES = 56 * 1024 * 1024

LOG2_E = 1.4426950408889634
HGRN_CHUNK = 128
HGRN_MATMUL_LEVEL_LIMIT = 8
HGRN_ROWS_PER_STEP = 512
V_AUG_ROWS = 80
MOBA_HEADS_PER_STEP = 4


def _rmsnorm(x, w):
    return x * lax.rsqrt(jnp.mean(x * x, axis=-1, keepdims=True) + NORM_EPS) * w


def _silu(x):
    return x * jax.nn.sigmoid(x)


def _dot(a, b):
    return jnp.dot(a, b, preferred_element_type=F32)


def _dot_nt(a, b):
    return lax.dot_general(a, b, (((1,), (1,)), ((), ())), preferred_element_type=F32)


def _resident(shape):
    nd = len(shape)
    return pl.BlockSpec(shape, lambda *_: (0,) * nd, pipeline_mode=pl.Buffered(1))


def _params(*sem):
    return pltpu.CompilerParams(dimension_semantics=sem, vmem_limit_bytes=V7X_VMEM_LIMIT_BYTES)


def _ffn_kernel(x_ref, *refs, merge, final):
    refs = list(refs)
    o_ref = refs.pop()
    x = x_ref[...]
    if merge:
        oa_ref, ob_ref, g_ref, wa_ref, wb_ref, wo_ref = refs[:6]
        refs = refs[6:]
        d = x.shape[1]
        ga = jax.nn.sigmoid(g_ref[:, 0:d].astype(F32))
        gb = jax.nn.sigmoid(g_ref[:, d:2 * d].astype(F32))
        merged = ga * _dot(oa_ref[...], wa_ref[...]) + gb * _dot(ob_ref[...], wb_ref[...])
        x = x + _dot(merged.astype(BF16), wo_ref[...])
    nw_ref, wg_ref, wu_ref, wd_ref = refs[:4]
    h = _rmsnorm(x, nw_ref[...]).astype(BF16)
    g = _dot(h, wg_ref[...])
    u = _dot(h, wu_ref[...])
    a = (_silu(g) * u).astype(BF16)
    y = x + 0.5 * _dot(a, wd_ref[...])
    if final:
        y = _rmsnorm(y, refs[4][...])
    o_ref[...] = y


def _ffn(x, nw, wg, wu, wd, merge=None, final_w=None, tm=512):
    n, d = x.shape
    ff = wg.shape[1]

    def row(width):
        return pl.BlockSpec((tm, width), lambda i: (i, 0))

    in_specs, args = [row(d)], [x]
    if merge is not None:
        in_specs += [row(a.shape[1]) for a in merge[:3]] + [_resident(a.shape) for a in merge[3:]]
        args += list(merge)
    in_specs += [_resident((1, d)), _resident((d, ff)), _resident((d, ff)), _resident((ff, d))]
    args += [nw.reshape(1, d), wg, wu, wd]
    if final_w is not None:
        in_specs.append(_resident((1, d)))
        args.append(final_w.reshape(1, d))
    return pl.pallas_call(
        functools.partial(_ffn_kernel, merge=merge is not None, final=final_w is not None),
        grid=(n // tm,),
        in_specs=in_specs,
        out_specs=row(d),
        out_shape=jax.ShapeDtypeStruct((n, d), F32),
        compiler_params=_params("parallel"),
        name="ffn",
    )(*args)


def _inproj_kernel(x_ref, nw_ref, wh_ref, wk_ref, wqvt_ref, wg_ref, h_ref, k_ref, qt_ref, vt_ref, g_ref):
    h = _rmsnorm(x_ref[...], nw_ref[...]).astype(BF16)
    h_ref[...] = _dot(h, wh_ref[...])
    k_ref[...] = _dot(h, wk_ref[...]).astype(BF16)
    qvt = _dot_nt(wqvt_ref[...], h).astype(BF16)
    dq = qt_ref.shape[1]
    for blk in range(qt_ref.shape[0]):
        qt_ref[blk] = qvt[:dq, blk * MOBA_BLOCK:(blk + 1) * MOBA_BLOCK]
        vt_ref[blk] = qvt[dq:, blk * MOBA_BLOCK:(blk + 1) * MOBA_BLOCK]
    g_ref[...] = _dot(h, wg_ref[...]).astype(BF16)


def _inproj(x, nw, wh, wk, wqvt, wg, tm=512):
    n, d = x.shape
    dh, dk, dqv, dg = wh.shape[1], wk.shape[1], wqvt.shape[0], wg.shape[1]
    blocks = tm // MOBA_BLOCK
    transposed = pl.BlockSpec((blocks, dqv // 2, MOBA_BLOCK), lambda i: (i, 0, 0))
    return pl.pallas_call(
        _inproj_kernel,
        grid=(n // tm,),
        in_specs=[pl.BlockSpec((tm, d), lambda i: (i, 0)), _resident((1, d)),
                  _resident((d, dh)), _resident((d, dk)), _resident((dqv, d)), _resident((d, dg))],
        out_specs=[pl.BlockSpec((tm, dh), lambda i: (i, 0)),
                   pl.BlockSpec((tm, dk), lambda i: (i, 0)),
                   transposed, transposed,
                   pl.BlockSpec((tm, dg), lambda i: (i, 0))],
        out_shape=[jax.ShapeDtypeStruct((n, dh), F32),
                   jax.ShapeDtypeStruct((n, dk), BF16),
                   jax.ShapeDtypeStruct((n // MOBA_BLOCK, dqv // 2, MOBA_BLOCK), BF16),
                   jax.ShapeDtypeStruct((n // MOBA_BLOCK, dqv // 2, MOBA_BLOCK), BF16),
                   jax.ShapeDtypeStruct((n, dg), BF16)],
        compiler_params=_params("parallel"),
        name="inproj",
    )(x, nw.reshape(1, d), wh, wk, wqvt, wg)


def _hgrn_constants(c):
    t = np.arange(c)
    sp = t[None, :]
    tt = t[:, None]
    rows = [sp <= tt, sp > tt]
    masks, upper = [], []
    m = 1
    while m < c:
        up = (t // m) % 2 == 1
        bnd = ((t // (2 * m)) * 2 * m + m - 1)[:, None]
        if m < HGRN_MATMUL_LEVEL_LIMIT:
            rows.append(np.where(up[:, None], (sp > bnd) & (sp <= tt), (sp > tt) & (sp <= bnd)))
        same = (tt // (2 * m)) == (sp // (2 * m))
        masks.append(up[:, None] & (~up)[None, :] & same)
        upper.append(np.broadcast_to(up[:, None], (c, HGRN_DK)))
        m *= 2
    w = np.concatenate(rows, axis=0).astype(np.float32)
    return (jnp.asarray(w, BF16), jnp.asarray(np.stack(masks), F32),
            jnp.asarray(np.stack(upper), F32))


def _hgrn_kernel(lbl_ref, nw_ref, w_ref, lm_ref, up_ref, h_ref, o_ref, st_ref, *, layer, n_levels):
    c = HGRN_CHUNK
    dk = HGRN_DK
    kdim = HGRN_HEADS * dk

    @pl.when(pl.program_id(1) == 0)
    def _():
        st_ref[...] = jnp.zeros_like(st_ref)

    logits = lbl_ref[...]
    e = jnp.exp(logits - jnp.max(logits, axis=0, keepdims=True))
    sm = e / jnp.sum(e, axis=0, keepdims=True)
    cum = sm[0:1, :]
    for j in range(1, layer + 1):
        cum = cum + sm[j:j + 1, :]
    lb_all = jnp.clip(cum - sm[0:1, :], 0.0, 1.0)
    w = w_ref[...]
    nw = nw_ref[...]

    heads = [slice(h * dk, (h + 1) * dk) for h in range(HGRN_HEADS)]

    def chunk(ci, carry):
        r0 = pl.multiple_of(ci * c, c)
        q = _silu(h_ref[pl.ds(r0, c), 0:kdim])
        sig = jax.nn.sigmoid(h_ref[pl.ds(r0, c), kdim:2 * kdim])
        inp = h_ref[pl.ds(r0, c), 2 * kdim:3 * kdim]
        og = h_ref[pl.ds(r0, c), 3 * kdim:4 * kdim]
        f = lb_all + (1.0 - lb_all) * sig
        g = jnp.log(jnp.maximum(f, MIN_FORGET)) * LOG2_E
        k = (1.0 - lb_all) * (1.0 - sig)

        g_hi = g.astype(BF16)
        g_lo = (g - g_hi.astype(F32)).astype(BF16)
        r = _dot(w, jnp.concatenate([g_hi, g_lo], axis=1))
        e = r[:, :kdim] + r[:, kdim:]
        x = jnp.exp2(e)
        b = e[0:c]

        qd = (q * x[0:c]).astype(BF16)
        kd = (k * x[c:2 * c]).astype(BF16)
        decay_all = x[c - 1:c, :]
        dqk = q - k
        a = [jnp.zeros((c, c), F32) for _ in heads]
        for l in range(n_levels):
            m = 2 ** l
            if m < HGRN_MATMUL_LEVEL_LIMIT:
                xl = x[(2 + l) * c:(3 + l) * c]
            else:
                b3 = b.reshape(c // (2 * m), 2 * m, kdim)
                xl = jnp.exp2(-jnp.abs(b3 - b3[:, m - 1:m, :])).reshape(c, kdim)
            up = jnp.concatenate([up_ref[l]] * HGRN_HEADS, axis=1)
            z = ((k + up * dqk) * xl).astype(BF16)
            mask = lm_ref[l]
            a = [a[h] + mask * _dot_nt(z[:, hs], z[:, hs]) for h, hs in enumerate(heads)]
        inp_b = inp.astype(BF16)
        qk = q * k
        outs = []
        for h, hs in enumerate(heads):
            st = st_ref[h]
            o = (_dot(a[h].astype(BF16), inp_b[:, hs]) + _dot_nt(qd[:, hs], st.astype(BF16))
                 + jnp.sum(qk[:, hs], axis=-1, keepdims=True) * inp[:, hs])
            st_ref[h] = st * decay_all[:, hs] + _dot(inp[:, hs].T.astype(BF16), kd[:, hs])
            outs.append(o * lax.rsqrt(jnp.mean(o * o, axis=-1, keepdims=True) + NORM_EPS) * nw)
        o_ref[pl.ds(r0, c), :] = (jnp.concatenate(outs, axis=1) * _silu(og)).astype(o_ref.dtype)
        return carry

    lax.fori_loop(0, h_ref.shape[0] // c, chunk, 0)


def _hgrn(h4, lb_logits, norm_w, layer, batch, seq):
    n = h4.shape[0]
    c = HGRN_CHUNK
    tb = HGRN_ROWS_PER_STEP
    steps = seq // tb
    w, lm, up = _hgrn_constants(c)
    n_levels = lm.shape[0]
    vdim = HGRN_HEADS * HGRN_DK
    return pl.pallas_call(
        functools.partial(_hgrn_kernel, layer=layer, n_levels=n_levels),
        grid=(batch, steps),
        in_specs=[_resident(lb_logits.shape), _resident((1, HGRN_DK)), _resident(w.shape),
                  _resident(lm.shape), _resident(up.shape),
                  pl.BlockSpec((tb, h4.shape[1]), lambda b, s: (b * steps + s, 0))],
        out_specs=pl.BlockSpec((tb, vdim), lambda b, s: (b * steps + s, 0)),
        out_shape=jax.ShapeDtypeStruct((n, vdim), BF16),
        scratch_shapes=[pltpu.VMEM((HGRN_HEADS, HGRN_DK, HGRN_DK), F32)],
        compiler_params=_params("parallel", "arbitrary"),
        name="hgrn",
    )(lb_logits, norm_w.reshape(1, HGRN_DK), w, lm, up, h4)


def _moba_kernel(slope_ref, tab_ref, qt_ref, k_ref, vt_ref, pb_ref, o_ref, km_ref, sel_ref, s_ref, *, nb, n_steps):
    bs = MOBA_BLOCK
    dh = MOBA_HEAD_DIM
    na = MOBA_HEADS_PER_STEP
    width = na * dh
    hg = pl.program_id(1)

    for j in range(nb):
        km_ref[j:j + 1, :] = jnp.mean(k_ref[j * bs:(j + 1) * bs, :].astype(F32), axis=0, keepdims=True)

    lane = lax.broadcasted_iota(jnp.int32, (1, width), 1)
    row = lax.broadcasted_iota(jnp.int32, (width, 1), 0)
    head_lanes = [(lane >= a * dh) & (lane < (a + 1) * dh) for a in range(na)]
    head_rows = [(row >= a * dh) & (row < (a + 1) * dh) for a in range(na)]

    km = km_ref[...]
    km2 = jnp.concatenate([jnp.where(head_lanes[a], km, 0.0) for a in range(na)], axis=0)
    p0 = km2.astype(BF16)
    r1 = km2 - p0.astype(F32)
    p1 = r1.astype(BF16)
    p2 = (r1 - p1.astype(F32)).astype(BF16)
    jidx = lax.broadcasted_iota(jnp.int32, (nb, bs), 0)
    kt = min(MOBA_TOPK, nb)

    def select(i, carry):
        qt = qt_ref[i]
        gate2 = _dot(p0, qt) + _dot(p1, qt) + _dot(p2, qt)
        past = jidx < i
        for a in range(na):
            gm = jnp.where(past, gate2[a * nb:(a + 1) * nb, :], MASK_VALUE)
            cnt = jnp.zeros((nb, bs), F32)
            for jp in range(nb):
                row = gm[jp:jp + 1, :]
                beats = (row > gm) | ((row == gm) & (jp < jidx))
                cnt = cnt + jnp.where(beats, 1.0, 0.0) * (jp < i).astype(F32)
            sel_ref[i, a] = jnp.where((past & (cnt < kt)) | (jidx == i), 1.0, 0.0)
        return carry

    lax.fori_loop(0, nb, select, 0)

    scale = dh ** -0.5
    slopes = [slope_ref[na * hg + a] for a in range(na)]

    def block_ids(i, t):
        ids = []
        for u in range(2):
            b = i - 2 * t - u
            ids.append((jnp.maximum(b, 0), jnp.where(b >= 0, 1.0, 0.0)))
        return ids

    def scores(i, t):
        qt = qt_ref[i]
        qm = [jnp.where(head_rows[a], qt, jnp.zeros_like(qt)) * jnp.asarray(scale, BF16) for a in range(na)]
        s_all, stats = [], []
        for u, (bc, valid) in enumerate(block_ids(i, t)):
            k_b = k_ref[pl.ds(pl.multiple_of(bc * bs, bs), bs), :]
            diag = jnp.where(bc == i, 1, 0)
            for a in range(na):
                s = _dot(k_b, qm[a]) + pb_ref[diag, a]
                chosen = sel_ref[i, a, pl.ds(bc, 1), :] * valid > 0.5
                off = jnp.where(chosen, slopes[a] * (bc.astype(F32) * float(bs)), -jnp.inf)
                s_all.append(s)
                stats.append((jnp.max(s, axis=0, keepdims=True) + off, off))
        return s_all, tuple(stats)

    ones_rows = jnp.ones((V_AUG_ROWS - dh, 2 * bs), BF16)

    def step(n, carry):
        stats, state = carry
        i, t = tab_ref[0, n], tab_ref[1, n]
        s_cur = [s_ref[ua] for ua in range(2 * na)]
        s_nxt, stats_nxt = scores(tab_ref[0, n + 1], tab_ref[1, n + 1])
        ids = block_ids(i, t)
        first = t == 0
        cap = jnp.where(first, -jnp.inf, jnp.inf)
        keep = jnp.where(first, 0.0, 1.0)
        out = []
        for a in range(na):
            m_old = jnp.minimum(state[a][0], cap)
            m_new = jnp.maximum(m_old, jnp.maximum(stats[a][0], stats[na + a][0]))
            p = [jnp.exp((s_cur[u * na + a] - (m_new - stats[u * na + a][1])).astype(BF16)) for u in range(2)]
            v = jnp.concatenate([vt_ref[ids[u][0], a * dh:(a + 1) * dh, :] for u in range(2)], axis=1)
            pv = _dot(jnp.concatenate([v, ones_rows], axis=0), jnp.concatenate(p, axis=0))
            out.append((m_new, (jnp.exp(m_old - m_new) * keep) * state[a][1] + pv))
        for ua in range(2 * na):
            s_ref[ua] = s_nxt[ua]

        @pl.when(t == (i + 2) // 2 - 1)
        def _():
            o = [out[a][1][0:dh] / out[a][1][dh:dh + 1] for a in range(na)]
            o_ref[pl.ds(pl.multiple_of(i * bs, bs), bs), :] = jnp.concatenate(o, axis=0).T.astype(o_ref.dtype)

        return stats_nxt, tuple(out)

    s0, stats0 = scores(tab_ref[0, 0], tab_ref[1, 0])
    for ua in range(2 * na):
        s_ref[ua] = s0[ua]
    init = tuple((jnp.full((1, bs), -jnp.inf, F32), jnp.zeros((V_AUG_ROWS, bs), F32)) for _ in range(na))
    lax.fori_loop(0, n_steps, step, (stats0, init))


def _moba(qt, k, vt, batch, seq):
    n = k.shape[0]
    bs, dh, nh, na = MOBA_BLOCK, MOBA_HEAD_DIM, MOBA_HEADS, MOBA_HEADS_PER_STEP
    nb = seq // bs
    dim = nh * dh
    slopes = jnp.exp2(-8.0 * jnp.arange(1, nh + 1, dtype=F32) / nh)
    key_pos = jnp.arange(bs, dtype=F32)[:, None]
    pos_bias = slopes[:, None, None] * jnp.broadcast_to(key_pos, (bs, bs))[None]
    causal = jnp.arange(bs)[:, None] <= jnp.arange(bs)[None, :]
    pos_bias = jnp.stack([pos_bias, jnp.where(causal[None], pos_bias, -jnp.inf)])
    table = [(i, t) for i in range(nb) for t in range((i + 2) // 2)] + [(0, 0)]
    table = jnp.asarray(np.array(table, np.int32).T)
    smem = pl.BlockSpec(memory_space=pltpu.SMEM)
    transposed = pl.BlockSpec((nb, na * dh, bs), lambda b, hg: (b, hg, 0))
    return pl.pallas_call(
        functools.partial(_moba_kernel, nb=nb, n_steps=table.shape[1] - 1),
        grid=(batch, nh // na),
        in_specs=[smem, smem, transposed,
                  pl.BlockSpec((seq, na * dh), lambda b, hg: (b, hg)),
                  transposed,
                  pl.BlockSpec((2, na, bs, bs), lambda b, hg: (0, hg, 0, 0))],
        out_specs=pl.BlockSpec((seq, na * dh), lambda b, hg: (b, hg)),
        out_shape=jax.ShapeDtypeStruct((n, dim), BF16),
        scratch_shapes=[pltpu.VMEM((nb, na * dh), F32), pltpu.VMEM((nb, na, nb, bs), F32),
                        pltpu.VMEM((2 * na, bs, bs), F32)],
        compiler_params=_params("parallel", "parallel"),
        name="moba",
    )(slopes, table, qt, k, vt, pos_bias)


def kernel(x, ffn1_norm, ffn1_w_gu, ffn1_w_down, mix_norm, w_in, hgrn_lb_logits, hgrn_norm,
           w_branch_a, w_branch_b, w_out, ffn2_norm, ffn2_w_gu, ffn2_w_down, final_norm):
    batch, seq, d = x.shape
    depth = ffn1_norm.shape[0]
    ff = ffn1_w_down.shape[1]
    hdim = 4 * HGRN_HEADS * HGRN_DK
    mdim = MOBA_HEADS * MOBA_HEAD_DIM
    xf = x.reshape(batch * seq, d)
    for l in range(depth):
        wgu = ffn1_w_gu[l].astype(BF16)
        xf = _ffn(xf, ffn1_norm[l], wgu[:, :ff], wgu[:, ff:], ffn1_w_down[l].astype(BF16))
        wi = w_in[l].astype(BF16)
        w_q, w_k, w_v = (wi[:, hdim + j * mdim:hdim + (j + 1) * mdim] for j in range(3))
        h4, k, qt, vt, gates = _inproj(xf, mix_norm[l], wi[:, :hdim], w_k,
                                       jnp.concatenate([w_q, w_v], axis=1).T, wi[:, hdim + 3 * mdim:])
        oa = _hgrn(h4, hgrn_lb_logits, hgrn_norm[l], l, batch, seq)
        ob = _moba(qt, k, vt, batch, seq)
        merge = (oa, ob, gates, w_branch_a[l].astype(BF16), w_branch_b[l].astype(BF16), w_out[l].astype(BF16))
        wgu = ffn2_w_gu[l].astype(BF16)
        xf = _ffn(xf, ffn2_norm[l], wgu[:, :ff], wgu[:, ff:], ffn2_w_down[l].astype(BF16),
                  merge=merge, final_w=final_norm if l == depth - 1 else None)
    return xf.reshape(batch, seq, d)
```

```python
import functools

import numpy as np
import jax
import jax.numpy as jnp
from jax import lax
from jax.experimental import pallas as pl
from jax.experimental.pallas import tpu as pltpu

F32 = jnp.float32
BF16 = jnp.bfloat16

NORM_EPS = 1e-6
MIN_FORGET = 1e-20
MASK_VALUE = -1e9

HGRN_HEADS = 4
HGRN_DK = 128
MOBA_HEADS = 8
MOBA_HEAD_DIM = 64
MOBA_BLOCK = 256
MOBA_TOPK = 3

V7X_VMEM_LIMIT_BYTES = 56 * 1024 * 1024

LOG2_E = 1.4426950408889634
HGRN_CHUNK = 128
HGRN_MATMUL_LEVEL_LIMIT = 8
HGRN_ROWS_PER_STEP = 1024
V_AUG_ROWS = 80
MOBA_HEADS_PER_STEP = 4
MOBA_BLOCKS_PER_STEP = 2


def _rmsnorm(x, w):
    return x * lax.rsqrt(jnp.mean(x * x, axis=-1, keepdims=True) + NORM_EPS) * w


def _silu(x):
    return x * jax.nn.sigmoid(x)


def _dot(a, b):
    return jnp.dot(a, b, preferred_element_type=F32)


def _dot_nt(a, b):
    return lax.dot_general(a, b, (((1,), (1,)), ((), ())), preferred_element_type=F32)


def _resident(shape):
    nd = len(shape)
    return pl.BlockSpec(shape, lambda *_: (0,) * nd, pipeline_mode=pl.Buffered(1))


def _resident_layer(stacked, layer):
    return pl.BlockSpec((None,) + stacked.shape[1:], lambda *_: (layer, 0, 0), pipeline_mode=pl.Buffered(1))


def _params(*sem):
    return pltpu.CompilerParams(dimension_semantics=sem, vmem_limit_bytes=V7X_VMEM_LIMIT_BYTES)


def _ffn_kernel(x_ref, *refs, merge, final):
    refs = list(refs)
    o_ref = refs.pop()
    x = x_ref[...]
    if merge:
        oa_ref, ob_ref, g_ref, wa_ref, wb_ref, wo_ref = refs[:6]
        refs = refs[6:]
        d = x.shape[1]
        ga = jax.nn.sigmoid(g_ref[:, 0:d].astype(F32))
        gb = jax.nn.sigmoid(g_ref[:, d:2 * d].astype(F32))
        merged = ga * _dot(oa_ref[...], wa_ref[...]) + gb * _dot(ob_ref[...], wb_ref[...])
        x = x + _dot(merged.astype(BF16), wo_ref[...])
    nw_ref, wgu_ref, wd_ref = refs[:3]
    ff = wd_ref.shape[0]
    h = _rmsnorm(x, nw_ref[...]).astype(BF16)
    g = _dot(h, wgu_ref[:, :ff])
    u = _dot(h, wgu_ref[:, ff:])
    a = (_silu(g) * u).astype(BF16)
    y = x + 0.5 * _dot(a, wd_ref[...])
    if final:
        y = _rmsnorm(y, refs[3][...])
    o_ref[...] = y


def _ffn(x, nw, wgu, wd, layer, merge=None, final_w=None, tm=512):
    n, d = x.shape

    def row(width):
        return pl.BlockSpec((tm, width), lambda i: (i, 0))

    in_specs, args = [row(d)], [x]
    if merge is not None:
        in_specs += [row(a.shape[1]) for a in merge[:3]] + [_resident_layer(a, layer) for a in merge[3:]]
        args += list(merge)
    in_specs += [_resident((1, d)), _resident_layer(wgu, layer), _resident_layer(wd, layer)]
    args += [nw.reshape(1, d), wgu, wd]
    if final_w is not None:
        in_specs.append(_resident((1, d)))
        args.append(final_w.reshape(1, d))
    return pl.pallas_call(
        functools.partial(_ffn_kernel, merge=merge is not None, final=final_w is not None),
        grid=(n // tm,),
        in_specs=in_specs,
        out_specs=row(d),
        out_shape=jax.ShapeDtypeStruct((n, d), F32),
        compiler_params=_params("parallel"),
        name="ffn",
    )(*args)


def _inproj_kernel(x_ref, nw_ref, wi_ref, wqvt_ref, h_ref, k_ref, qt_ref, vt_ref, g_ref):
    dh, dk, dg = h_ref.shape[1], k_ref.shape[1], g_ref.shape[1]
    k0 = dh + dk
    h = _rmsnorm(x_ref[...], nw_ref[...]).astype(BF16)
    h_ref[...] = _dot(h, wi_ref[:, :dh])
    k_ref[...] = _dot(h, wi_ref[:, k0:k0 + dk]).astype(BF16)
    qvt = _dot_nt(wqvt_ref[...], h).astype(BF16)
    dq = qt_ref.shape[1]
    for blk in range(qt_ref.shape[0]):
        qt_ref[blk] = qvt[:dq, blk * MOBA_BLOCK:(blk + 1) * MOBA_BLOCK]
        vt_ref[blk] = qvt[dq:, blk * MOBA_BLOCK:(blk + 1) * MOBA_BLOCK]
    g_ref[...] = _dot(h, wi_ref[:, wi_ref.shape[1] - dg:]).astype(BF16)


def _inproj(x, nw, wi, wqvt, layer, tm=512):
    n, d = x.shape
    dqv = wqvt.shape[0]
    dk = dqv // 2
    dh = 4 * HGRN_HEADS * HGRN_DK
    dg = wi.shape[2] - dh - 3 * dk
    blocks = tm // MOBA_BLOCK
    transposed = pl.BlockSpec((blocks, dqv // 2, MOBA_BLOCK), lambda i: (i, 0, 0))
    return pl.pallas_call(
        _inproj_kernel,
        grid=(n // tm,),
        in_specs=[pl.BlockSpec((tm, d), lambda i: (i, 0)), _resident((1, d)),
                  _resident_layer(wi, layer), _resident((dqv, d))],
        out_specs=[pl.BlockSpec((tm, dh), lambda i: (i, 0)),
                   pl.BlockSpec((tm, dk), lambda i: (i, 0)),
                   transposed, transposed,
                   pl.BlockSpec((tm, dg), lambda i: (i, 0))],
        out_shape=[jax.ShapeDtypeStruct((n, dh), F32),
                   jax.ShapeDtypeStruct((n, dk), BF16),
                   jax.ShapeDtypeStruct((n // MOBA_BLOCK, dqv // 2, MOBA_BLOCK), BF16),
                   jax.ShapeDtypeStruct((n // MOBA_BLOCK, dqv // 2, MOBA_BLOCK), BF16),
                   jax.ShapeDtypeStruct((n, dg), BF16)],
        compiler_params=_params("parallel"),
        name="inproj",
    )(x, nw.reshape(1, d), wi, wqvt)


def _hgrn_constants(c):
    t = np.arange(c)
    sp = t[None, :]
    tt = t[:, None]
    rows = [sp <= tt, sp > tt]
    masks, upper = [], []
    m = 1
    while m < c:
        up = (t // m) % 2 == 1
        bnd = ((t // (2 * m)) * 2 * m + m - 1)[:, None]
        if m < HGRN_MATMUL_LEVEL_LIMIT:
            rows.append(np.where(up[:, None], (sp > bnd) & (sp <= tt), (sp > tt) & (sp <= bnd)))
        same = (tt // (2 * m)) == (sp // (2 * m))
        masks.append(up[:, None] & (~up)[None, :] & same)
        upper.append(np.broadcast_to(up[:, None], (c, HGRN_DK)))
        m *= 2
    w = np.concatenate(rows, axis=0).astype(np.float32)
    return (jnp.asarray(w, BF16), jnp.asarray(np.stack(masks), F32),
            jnp.asarray(np.stack(upper), F32))


def _hgrn_kernel(lbl_ref, nw_ref, w_ref, lm_ref, up_ref, h_ref, o_ref, st_ref, *, layer, n_levels):
    c = HGRN_CHUNK
    dk = HGRN_DK
    kdim = HGRN_HEADS * dk

    @pl.when(pl.program_id(1) == 0)
    def _():
        st_ref[...] = jnp.zeros_like(st_ref)

    logits = lbl_ref[...]
    e = jnp.exp(logits - jnp.max(logits, axis=0, keepdims=True))
    sm = e / jnp.sum(e, axis=0, keepdims=True)
    cum = sm[0:1, :]
    for j in range(1, layer + 1):
        cum = cum + sm[j:j + 1, :]
    lb_all = jnp.clip(cum - sm[0:1, :], 0.0, 1.0)
    w = w_ref[...]
    nw = nw_ref[...]

    heads = [slice(h * dk, (h + 1) * dk) for h in range(HGRN_HEADS)]

    def chunk(ci, carry):
        r0 = pl.multiple_of(ci * c, c)
        q = _silu(h_ref[pl.ds(r0, c), 0:kdim])
        sig = jax.nn.sigmoid(h_ref[pl.ds(r0, c), kdim:2 * kdim])
        inp = h_ref[pl.ds(r0, c), 2 * kdim:3 * kdim]
        og = h_ref[pl.ds(r0, c), 3 * kdim:4 * kdim]
        f = lb_all + (1.0 - lb_all) * sig
        g = jnp.log(jnp.maximum(f, MIN_FORGET)) * LOG2_E
        k = (1.0 - lb_all) * (1.0 - sig)

        g_hi = g.astype(BF16)
        g_lo = (g - g_hi.astype(F32)).astype(BF16)
        r = _dot(w, jnp.concatenate([g_hi, g_lo], axis=1))
        e = r[:, :kdim] + r[:, kdim:]
        x = jnp.exp2(e)
        b = e[0:c]

        qd = (q * x[0:c]).astype(BF16)
        kd = (k * x[c:2 * c]).astype(BF16)
        decay_all = x[c - 1:c, :]
        dqk = q - k
        a = [jnp.zeros((c, c), F32) for _ in heads]
        for l in range(n_levels):
            m = 2 ** l
            if m < HGRN_MATMUL_LEVEL_LIMIT:
                up = jnp.concatenate([up_ref[l]] * HGRN_HEADS, axis=1)
                z = ((k + up * dqk) * x[(2 + l) * c:(3 + l) * c]).astype(BF16)
            else:
                b3 = b.reshape(c // (2 * m), 2 * m, kdim)
                d = (b3 - b3[:, m - 1:m, :]).reshape(c, kdim)
                blocks = [slice(j * m, (j + 1) * m) for j in range(c // m)]
                e_l = jnp.concatenate([d[rs] if j % 2 else -d[rs] for j, rs in enumerate(blocks)], axis=0)
                qk_rows = jnp.concatenate([q[rs] if j % 2 else k[rs] for j, rs in enumerate(blocks)], axis=0)
                z = (qk_rows * jnp.exp2(e_l)).astype(BF16)
            mask = lm_ref[l]
            a = [a[h] + mask * _dot_nt(z[:, hs], z[:, hs]) for h, hs in enumerate(heads)]
        inp_b = inp.astype(BF16)
        qk = q * k
        outs = []
        for h, hs in enumerate(heads):
            st = st_ref[h]
            o = (_dot(a[h].astype(BF16), inp_b[:, hs]) + _dot_nt(qd[:, hs], st.astype(BF16))
                 + jnp.sum(qk[:, hs], axis=-1, keepdims=True) * inp[:, hs])
            st_ref[h] = st * decay_all[:, hs] + _dot(inp[:, hs].T.astype(BF16), kd[:, hs])
            outs.append(o * lax.rsqrt(jnp.mean(o * o, axis=-1, keepdims=True) + NORM_EPS) * nw)
        o_ref[pl.ds(r0, c), :] = (jnp.concatenate(outs, axis=1) * _silu(og)).astype(o_ref.dtype)
        return carry

    lax.fori_loop(0, h_ref.shape[0] // c, chunk, 0, unroll=True)


def _hgrn(h4, lb_logits, norm_w, layer, batch, seq):
    n = h4.shape[0]
    c = HGRN_CHUNK
    tb = HGRN_ROWS_PER_STEP
    steps = seq // tb
    w, lm, up = _hgrn_constants(c)
    n_levels = lm.shape[0]
    vdim = HGRN_HEADS * HGRN_DK
    return pl.pallas_call(
        functools.partial(_hgrn_kernel, layer=layer, n_levels=n_levels),
        grid=(batch, steps),
        in_specs=[_resident(lb_logits.shape), _resident((1, HGRN_DK)), _resident(w.shape),
                  _resident(lm.shape), _resident(up.shape),
                  pl.BlockSpec((tb, h4.shape[1]), lambda b, s: (b * steps + s, 0))],
        out_specs=pl.BlockSpec((tb, vdim), lambda b, s: (b * steps + s, 0)),
        out_shape=jax.ShapeDtypeStruct((n, vdim), BF16),
        scratch_shapes=[pltpu.VMEM((HGRN_HEADS, HGRN_DK, HGRN_DK), F32)],
        compiler_params=_params("parallel", "arbitrary"),
        name="hgrn",
    )(lb_logits, norm_w.reshape(1, HGRN_DK), w, lm, up, h4)


def _moba_kernel(slope_ref, tab_ref, qt_ref, k_ref, vt_ref, pb_ref, o_ref, km_ref, sel_ref, s_ref, *, nb, n_steps):
    bs = MOBA_BLOCK
    dh = MOBA_HEAD_DIM
    na = MOBA_HEADS_PER_STEP
    nu = MOBA_BLOCKS_PER_STEP
    width = na * dh
    hg = pl.program_id(1)

    for j in range(nb):
        km_ref[j:j + 1, :] = jnp.mean(k_ref[j * bs:(j + 1) * bs, :].astype(F32), axis=0, keepdims=True)

    lane = lax.broadcasted_iota(jnp.int32, (1, width), 1)
    row = lax.broadcasted_iota(jnp.int32, (width, 1), 0)
    head_lanes = [(lane >= a * dh) & (lane < (a + 1) * dh) for a in range(na)]
    head_rows = [(row >= a * dh) & (row < (a + 1) * dh) for a in range(na)]

    km = km_ref[...]
    km2 = jnp.concatenate([jnp.where(head_lanes[a], km, 0.0) for a in range(na)], axis=0)
    p0 = km2.astype(BF16)
    r1 = km2 - p0.astype(F32)
    p1 = r1.astype(BF16)
    p2 = (r1 - p1.astype(F32)).astype(BF16)
    jidx = lax.broadcasted_iota(jnp.int32, (nb, bs), 0)
    kt = min(MOBA_TOPK, nb)

    def select(i, carry):
        qt = qt_ref[i]
        gate2 = _dot(p0, qt) + _dot(p1, qt) + _dot(p2, qt)
        past = jidx < i
        for a in range(na):
            gm = jnp.where(past, gate2[a * nb:(a + 1) * nb, :], MASK_VALUE)
            cnt = jnp.zeros((nb, bs), F32)
            for jp in range(nb):
                row = gm[jp:jp + 1, :]
                beats = (row > gm) | ((row == gm) & (jp < jidx))
                cnt = cnt + jnp.where(beats, 1.0, 0.0) * jnp.where(jp < i, 1.0, 0.0)
            sel_ref[i, a] = jnp.where((past & (cnt < kt)) | (jidx == i), 1.0, 0.0)
        return carry

    lax.fori_loop(0, nb, select, 0, unroll=2)

    scale = dh ** -0.5
    slopes = [slope_ref[na * hg + a] for a in range(na)]

    def block_ids(i, t):
        ids = []
        for u in range(nu):
            b = i - nu * t - u
            ids.append((jnp.maximum(b, 0), jnp.where(b >= 0, 1.0, 0.0)))
        return ids

    def scores(i, t):
        qt = qt_ref[i]
        qm = [jnp.where(head_rows[a], qt, jnp.zeros_like(qt)) * jnp.asarray(scale, BF16) for a in range(na)]
        s_all, stats = [], []
        for u, (bc, valid) in enumerate(block_ids(i, t)):
            k_b = k_ref[pl.ds(pl.multiple_of(bc * bs, bs), bs), :]
            diag = jnp.where(bc == i, 1, 0)
            for a in range(na):
                s = _dot(k_b, qm[a]) + pb_ref[diag, a]
                chosen = sel_ref[i, a, pl.ds(bc, 1), :] * valid > 0.5
                off = jnp.where(chosen, slopes[a] * (bc.astype(F32) * float(bs)), -jnp.inf)
                s_all.append(s)
                stats.append((jnp.max(s, axis=0, keepdims=True) + off, off))
        return s_all, tuple(stats)

    ones_rows = jnp.ones((V_AUG_ROWS - dh, nu * bs), BF16)

    def step(n, carry):
        stats, state = carry
        i, t = tab_ref[0, n], tab_ref[1, n]
        s_cur = [s_ref[ua] for ua in range(nu * na)]
        s_nxt, stats_nxt = scores(tab_ref[0, n + 1], tab_ref[1, n + 1])
        ids = block_ids(i, t)
        first = t == 0
        cap = jnp.where(first, -jnp.inf, jnp.inf)
        keep = jnp.where(first, 0.0, 1.0)
        out = []
        for a in range(na):
            m_old = jnp.minimum(state[a][0], cap)
            m_new = m_old
            for u in range(nu):
                m_new = jnp.maximum(m_new, stats[u * na + a][0])
            p = [jnp.exp((s_cur[u * na + a] - (m_new - stats[u * na + a][1])).astype(BF16)) for u in range(nu)]
            v = jnp.concatenate([vt_ref[ids[u][0], a * dh:(a + 1) * dh, :] for u in range(nu)], axis=1)
            pv = _dot(jnp.concatenate([v, ones_rows], axis=0), jnp.concatenate(p, axis=0))
            out.append((m_new, (jnp.exp(m_old - m_new) * keep) * state[a][1] + pv))
        for ua in range(nu * na):
            s_ref[ua] = s_nxt[ua]

        @pl.when(t == (i + nu) // nu - 1)
        def _():
            o = [out[a][1][0:dh] / out[a][1][dh:dh + 1] for a in range(na)]
            o_ref[pl.ds(pl.multiple_of(i * bs, bs), bs), :] = jnp.concatenate(o, axis=0).T.astype(o_ref.dtype)

        return stats_nxt, tuple(out)

    s0, stats0 = scores(tab_ref[0, 0], tab_ref[1, 0])
    for ua in range(nu * na):
        s_ref[ua] = s0[ua]
    init = tuple((jnp.full((1, bs), -jnp.inf, F32), jnp.zeros((V_AUG_ROWS, bs), F32)) for _ in range(na))
    lax.fori_loop(0, n_steps, step, (stats0, init), unroll=4)


def _moba(qt, k, vt, batch, seq):
    n = k.shape[0]
    bs, dh, nh, na = MOBA_BLOCK, MOBA_HEAD_DIM, MOBA_HEADS, MOBA_HEADS_PER_STEP
    nb = seq // bs
    dim = nh * dh
    slopes = jnp.exp2(-8.0 * jnp.arange(1, nh + 1, dtype=F32) / nh)
    key_pos = jnp.arange(bs, dtype=F32)[:, None]
    pos_bias = slopes[:, None, None] * jnp.broadcast_to(key_pos, (bs, bs))[None]
    causal = jnp.arange(bs)[:, None] <= jnp.arange(bs)[None, :]
    pos_bias = jnp.stack([pos_bias, jnp.where(causal[None], pos_bias, -jnp.inf)])
    nu = MOBA_BLOCKS_PER_STEP
    table = [(i, t) for i in range(nb) for t in range((i + nu) // nu)] + [(0, 0)]
    table = jnp.asarray(np.array(table, np.int32).T)
    smem = pl.BlockSpec(memory_space=pltpu.SMEM)
    transposed = pl.BlockSpec((nb, na * dh, bs), lambda b, hg: (b, hg, 0))
    return pl.pallas_call(
        functools.partial(_moba_kernel, nb=nb, n_steps=table.shape[1] - 1),
        grid=(batch, nh // na),
        in_specs=[smem, smem, transposed,
                  pl.BlockSpec((seq, na * dh), lambda b, hg: (b, hg)),
                  transposed,
                  pl.BlockSpec((2, na, bs, bs), lambda b, hg: (0, hg, 0, 0))],
        out_specs=pl.BlockSpec((seq, na * dh), lambda b, hg: (b, hg)),
        out_shape=jax.ShapeDtypeStruct((n, dim), BF16),
        scratch_shapes=[pltpu.VMEM((nb, na * dh), F32), pltpu.VMEM((nb, na, nb, bs), F32),
                        pltpu.VMEM((nu * na, bs, bs), F32)],
        compiler_params=_params("parallel", "parallel"),
        name="moba",
    )(slopes, table, qt, k, vt, pos_bias)


def kernel(x, ffn1_norm, ffn1_w_gu, ffn1_w_down, mix_norm, w_in, hgrn_lb_logits, hgrn_norm,
           w_branch_a, w_branch_b, w_out, ffn2_norm, ffn2_w_gu, ffn2_w_down, final_norm):
    batch, seq, d = x.shape
    depth = ffn1_norm.shape[0]
    hdim = 4 * HGRN_HEADS * HGRN_DK
    mdim = MOBA_HEADS * MOBA_HEAD_DIM
    xf = x.reshape(batch * seq, d)
    w1gu, w1d, w2gu, w2d, wi, wa, wb, wo = (w.astype(BF16) for w in (
        ffn1_w_gu, ffn1_w_down, ffn2_w_gu, ffn2_w_down, w_in, w_branch_a, w_branch_b, w_out))
    for l in range(depth):
        xf = _ffn(xf, ffn1_norm[l], w1gu, w1d, l)
        w_q = wi[l, :, hdim:hdim + mdim]
        w_v = wi[l, :, hdim + 2 * mdim:hdim + 3 * mdim]
        h4, k, qt, vt, gates = _inproj(xf, mix_norm[l], wi, jnp.concatenate([w_q, w_v], axis=1).T, l)
        oa = _hgrn(h4, hgrn_lb_logits, hgrn_norm[l], l, batch, seq)
        ob = _moba(qt, k, vt, batch, seq)
        xf = _ffn(xf, ffn2_norm[l], w2gu, w2d, l, merge=(oa, ob, gates, wa, wb, wo),
                  final_w=final_norm if l == depth - 1 else None)
    return xf.reshape(batch, seq, d)
```

```python
import functools

import numpy as np
import jax
import jax.numpy as jnp
from jax import lax
from jax.experimental import pallas as pl
from jax.experimental.pallas import tpu as pltpu

F32 = jnp.float32
BF16 = jnp.bfloat16

NORM_EPS = 1e-6
MIN_FORGET = 1e-20
MASK_VALUE = -1e9

HGRN_HEADS = 4
HGRN_DK = 128
MOBA_HEADS = 8
MOBA_HEAD_DIM = 64
MOBA_BLOCK = 256
MOBA_TOPK = 3

V7X_VMEM_LIMIT_BYTES = 56 * 1024 * 1024

LOG2_E = 1.4426950408889634
HGRN_CHUNK = 128
HGRN_MATMUL_LEVEL_LIMIT = 8
HGRN_ROWS_PER_STEP = 1024
V_AUG_ROWS = 80
MOBA_HEADS_PER_STEP = 4
MOBA_BLOCKS_PER_STEP = 2


def _rmsnorm(x, w):
    return x * lax.rsqrt(jnp.mean(x * x, axis=-1, keepdims=True) + NORM_EPS) * w


def _silu(x):
    return x * jax.nn.sigmoid(x)


def _dot(a, b):
    return jnp.dot(a, b, preferred_element_type=F32)


def _dot_nt(a, b):
    return lax.dot_general(a, b, (((1,), (1,)), ((), ())), preferred_element_type=F32)


def _resident(shape):
    nd = len(shape)
    return pl.BlockSpec(shape, lambda *_: (0,) * nd, pipeline_mode=pl.Buffered(1))


def _resident_layer(stacked, layer):
    return pl.BlockSpec((None,) + stacked.shape[1:], lambda *_: (layer, 0, 0), pipeline_mode=pl.Buffered(1))


def _cast_kernel(w_ref, o_ref):
    o_ref[...] = w_ref[...].astype(o_ref.dtype)


def _to_bf16(w):
    depth, r, c = w.shape
    rows = max(k for k in range(16, r + 1, 16) if r % k == 0 and k * c * 4 <= 6 * 1024 * 1024)
    block = pl.BlockSpec((1, rows, c), lambda l, i: (l, i, 0))
    return pl.pallas_call(
        _cast_kernel,
        grid=(depth, r // rows),
        in_specs=[block],
        out_specs=block,
        out_shape=jax.ShapeDtypeStruct(w.shape, BF16),
        compiler_params=_params("parallel", "parallel"),
        name="to_bf16",
    )(w)


def _params(*sem):
    return pltpu.CompilerParams(dimension_semantics=sem, vmem_limit_bytes=V7X_VMEM_LIMIT_BYTES)


def _ffn_kernel(x_ref, *refs, merge, final):
    refs = list(refs)
    o_ref = refs.pop()
    x = x_ref[...]
    if merge:
        oa_ref, ob_ref, g_ref, wa_ref, wb_ref, wo_ref = refs[:6]
        refs = refs[6:]
        d = x.shape[1]
        ga = jax.nn.sigmoid(g_ref[:, 0:d].astype(F32))
        gb = jax.nn.sigmoid(g_ref[:, d:2 * d].astype(F32))
        merged = ga * _dot(oa_ref[...], wa_ref[...]) + gb * _dot(ob_ref[...], wb_ref[...])
        x = x + _dot(merged.astype(BF16), wo_ref[...])
    nw_ref, wgu_ref, wd_ref = refs[:3]
    ff = wd_ref.shape[0]
    h = _rmsnorm(x, nw_ref[...]).astype(BF16)
    g = _dot(h, wgu_ref[:, :ff])
    u = _dot(h, wgu_ref[:, ff:])
    a = (_silu(g) * u).astype(BF16)
    y = x + 0.5 * _dot(a, wd_ref[...])
    if final:
        y = _rmsnorm(y, refs[3][...])
    o_ref[...] = y


def _ffn(x, nw, wgu, wd, layer, merge=None, final_w=None, tm=512):
    n, d = x.shape

    def row(width):
        return pl.BlockSpec((tm, width), lambda i: (i, 0))

    in_specs, args = [row(d)], [x]
    if merge is not None:
        in_specs += [row(a.shape[1]) for a in merge[:3]] + [_resident_layer(a, layer) for a in merge[3:]]
        args += list(merge)
    in_specs += [_resident((1, d)), _resident_layer(wgu, layer), _resident_layer(wd, layer)]
    args += [nw.reshape(1, d), wgu, wd]
    if final_w is not None:
        in_specs.append(_resident((1, d)))
        args.append(final_w.reshape(1, d))
    return pl.pallas_call(
        functools.partial(_ffn_kernel, merge=merge is not None, final=final_w is not None),
        grid=(n // tm,),
        in_specs=in_specs,
        out_specs=row(d),
        out_shape=jax.ShapeDtypeStruct((n, d), F32),
        compiler_params=_params("parallel"),
        name="ffn",
    )(*args)


def _inproj_kernel(x_ref, nw_ref, wi_ref, wqvt_ref, h_ref, k_ref, qt_ref, vt_ref, g_ref):
    dh, dk, dg = h_ref.shape[1], k_ref.shape[1], g_ref.shape[1]
    k0 = dh + dk
    h = _rmsnorm(x_ref[...], nw_ref[...]).astype(BF16)
    h_ref[...] = _dot(h, wi_ref[:, :dh])
    k_ref[...] = _dot(h, wi_ref[:, k0:k0 + dk]).astype(BF16)
    qvt = _dot_nt(wqvt_ref[...], h).astype(BF16)
    dq = qt_ref.shape[1]
    for blk in range(qt_ref.shape[0]):
        qt_ref[blk] = qvt[:dq, blk * MOBA_BLOCK:(blk + 1) * MOBA_BLOCK]
        vt_ref[blk] = qvt[dq:, blk * MOBA_BLOCK:(blk + 1) * MOBA_BLOCK]
    g_ref[...] = _dot(h, wi_ref[:, wi_ref.shape[1] - dg:]).astype(BF16)


def _inproj(x, nw, wi, wqvt, layer, tm=512):
    n, d = x.shape
    dqv = wqvt.shape[0]
    dk = dqv // 2
    dh = 4 * HGRN_HEADS * HGRN_DK
    dg = wi.shape[2] - dh - 3 * dk
    blocks = tm // MOBA_BLOCK
    transposed = pl.BlockSpec((blocks, dqv // 2, MOBA_BLOCK), lambda i: (i, 0, 0))
    return pl.pallas_call(
        _inproj_kernel,
        grid=(n // tm,),
        in_specs=[pl.BlockSpec((tm, d), lambda i: (i, 0)), _resident((1, d)),
                  _resident_layer(wi, layer), _resident((dqv, d))],
        out_specs=[pl.BlockSpec((tm, dh), lambda i: (i, 0)),
                   pl.BlockSpec((tm, dk), lambda i: (i, 0)),
                   transposed, transposed,
                   pl.BlockSpec((tm, dg), lambda i: (i, 0))],
        out_shape=[jax.ShapeDtypeStruct((n, dh), F32),
                   jax.ShapeDtypeStruct((n, dk), BF16),
                   jax.ShapeDtypeStruct((n // MOBA_BLOCK, dqv // 2, MOBA_BLOCK), BF16),
                   jax.ShapeDtypeStruct((n // MOBA_BLOCK, dqv // 2, MOBA_BLOCK), BF16),
                   jax.ShapeDtypeStruct((n, dg), BF16)],
        compiler_params=_params("parallel"),
        name="inproj",
    )(x, nw.reshape(1, d), wi, wqvt)


def _hgrn_constants(c):
    t = np.arange(c)
    sp = t[None, :]
    tt = t[:, None]
    rows = [sp <= tt, sp > tt]
    masks, upper = [], []
    m = 1
    while m < c:
        up = (t // m) % 2 == 1
        bnd = ((t // (2 * m)) * 2 * m + m - 1)[:, None]
        if m < HGRN_MATMUL_LEVEL_LIMIT:
            rows.append(np.where(up[:, None], (sp > bnd) & (sp <= tt), (sp > tt) & (sp <= bnd)))
        same = (tt // (2 * m)) == (sp // (2 * m))
        masks.append(up[:, None] & (~up)[None, :] & same)
        upper.append(np.broadcast_to(up[:, None], (c, HGRN_DK)))
        m *= 2
    w = np.concatenate(rows, axis=0).astype(np.float32)
    w = np.concatenate([w, w], axis=1)
    return (jnp.asarray(w, BF16), jnp.asarray(np.stack(masks), F32),
            jnp.asarray(np.stack(upper), F32))


def _hgrn_kernel(lbl_ref, nw_ref, w_ref, lm_ref, up_ref, h_ref, o_ref, st_ref, *, layer, n_levels):
    c = HGRN_CHUNK
    dk = HGRN_DK
    kdim = HGRN_HEADS * dk

    @pl.when(pl.program_id(1) == 0)
    def _():
        st_ref[...] = jnp.zeros_like(st_ref)

    logits = lbl_ref[...]
    e = jnp.exp(logits - jnp.max(logits, axis=0, keepdims=True))
    sm = e / jnp.sum(e, axis=0, keepdims=True)
    cum = sm[0:1, :]
    for j in range(1, layer + 1):
        cum = cum + sm[j:j + 1, :]
    lb_all = jnp.clip(cum - sm[0:1, :], 0.0, 1.0)
    w = w_ref[...]
    nw = nw_ref[...]

    heads = [slice(h * dk, (h + 1) * dk) for h in range(HGRN_HEADS)]

    def chunk(ci, carry):
        r0 = pl.multiple_of(ci * c, c)
        q = _silu(h_ref[pl.ds(r0, c), 0:kdim])
        sig = jax.nn.sigmoid(h_ref[pl.ds(r0, c), kdim:2 * kdim])
        inp = h_ref[pl.ds(r0, c), 2 * kdim:3 * kdim]
        og = h_ref[pl.ds(r0, c), 3 * kdim:4 * kdim]
        f = lb_all + (1.0 - lb_all) * sig
        g = jnp.log(jnp.maximum(f, MIN_FORGET)) * LOG2_E
        k = (1.0 - lb_all) * (1.0 - sig)

        g_hi = g.astype(BF16)
        g_lo = (g - g_hi.astype(F32)).astype(BF16)
        e = _dot(w, jnp.concatenate([g_hi, g_lo], axis=0))
        x = jnp.exp2(e)
        b = e[0:c]

        qd = (q * x[0:c]).astype(BF16)
        kd = (k * x[c:2 * c]).astype(BF16)
        decay_all = x[c - 1:c, :]
        dqk = q - k
        a = [jnp.zeros((c, c), F32) for _ in heads]
        for l in range(n_levels):
            m = 2 ** l
            if m < HGRN_MATMUL_LEVEL_LIMIT:
                up = jnp.concatenate([up_ref[l]] * HGRN_HEADS, axis=1)
                z = ((k + up * dqk) * x[(2 + l) * c:(3 + l) * c]).astype(BF16)
            else:
                b3 = b.reshape(c // (2 * m), 2 * m, kdim)
                d = (b3 - b3[:, m - 1:m, :]).reshape(c, kdim)
                blocks = [slice(j * m, (j + 1) * m) for j in range(c // m)]
                e_l = jnp.concatenate([d[rs] if j % 2 else -d[rs] for j, rs in enumerate(blocks)], axis=0)
                qk_rows = jnp.concatenate([q[rs] if j % 2 else k[rs] for j, rs in enumerate(blocks)], axis=0)
                z = (qk_rows * jnp.exp2(e_l)).astype(BF16)
            mask = lm_ref[l] > 0.5
            a = [jnp.where(mask, _dot_nt(z[:, hs], z[:, hs]), a[h]) for h, hs in enumerate(heads)]
        inp_b = inp.astype(BF16)
        qk = q * k
        outs = []
        for h, hs in enumerate(heads):
            st = st_ref[h]
            o = (_dot(a[h].astype(BF16), inp_b[:, hs]) + _dot_nt(qd[:, hs], st.astype(BF16))
                 + jnp.sum(qk[:, hs], axis=-1, keepdims=True) * inp[:, hs])
            st_ref[h] = st * decay_all[:, hs] + _dot(inp[:, hs].T.astype(BF16), kd[:, hs])
            outs.append(o * lax.rsqrt(jnp.mean(o * o, axis=-1, keepdims=True) + NORM_EPS) * nw)
        o_ref[pl.ds(r0, c), :] = (jnp.concatenate(outs, axis=1) * _silu(og)).astype(o_ref.dtype)
        return carry

    lax.fori_loop(0, h_ref.shape[0] // c, chunk, 0, unroll=True)


def _hgrn(h4, lb_logits, norm_w, layer, batch, seq):
    n = h4.shape[0]
    c = HGRN_CHUNK
    tb = HGRN_ROWS_PER_STEP
    steps = seq // tb
    w, lm, up = _hgrn_constants(c)
    n_levels = lm.shape[0]
    vdim = HGRN_HEADS * HGRN_DK
    return pl.pallas_call(
        functools.partial(_hgrn_kernel, layer=layer, n_levels=n_levels),
        grid=(batch, steps),
        in_specs=[_resident(lb_logits.shape), _resident((1, HGRN_DK)), _resident(w.shape),
                  _resident(lm.shape), _resident(up.shape),
                  pl.BlockSpec((tb, h4.shape[1]), lambda b, s: (b * steps + s, 0))],
        out_specs=pl.BlockSpec((tb, vdim), lambda b, s: (b * steps + s, 0)),
        out_shape=jax.ShapeDtypeStruct((n, vdim), BF16),
        scratch_shapes=[pltpu.VMEM((HGRN_HEADS, HGRN_DK, HGRN_DK), F32)],
        compiler_params=_params("parallel", "arbitrary"),
        name="hgrn",
    )(lb_logits, norm_w.reshape(1, HGRN_DK), w, lm, up, h4)


def _moba_kernel(slope_ref, tab_ref, qt_ref, k_ref, vt_ref, pb_ref, o_ref, km_ref, sel_ref, s_ref, acc_ref, *, nb, n_steps):
    bs = MOBA_BLOCK
    dh = MOBA_HEAD_DIM
    na = MOBA_HEADS_PER_STEP
    nu = MOBA_BLOCKS_PER_STEP
    width = na * dh
    hg = pl.program_id(1)

    for j in range(nb):
        km_ref[j:j + 1, :] = jnp.mean(k_ref[j * bs:(j + 1) * bs, :].astype(F32), axis=0, keepdims=True)

    lane = lax.broadcasted_iota(jnp.int32, (1, width), 1)
    row = lax.broadcasted_iota(jnp.int32, (width, 1), 0)
    head_lanes = [(lane >= a * dh) & (lane < (a + 1) * dh) for a in range(na)]
    head_rows = [(row >= a * dh) & (row < (a + 1) * dh) for a in range(na)]

    km = km_ref[...]
    km2 = jnp.concatenate([jnp.where(head_lanes[a], km, 0.0) for a in range(na)], axis=0)
    p0 = km2.astype(BF16)
    r1 = km2 - p0.astype(F32)
    p1 = r1.astype(BF16)
    p2 = (r1 - p1.astype(F32)).astype(BF16)
    jidx = lax.broadcasted_iota(jnp.int32, (nb, bs), 0)
    kt = min(MOBA_TOPK, nb)

    def select(i, carry):
        qt = qt_ref[i]
        gate2 = _dot(p0, qt) + _dot(p1, qt) + _dot(p2, qt)
        past = jidx < i
        for a in range(na):
            gm = jnp.where(past, gate2[a * nb:(a + 1) * nb, :], MASK_VALUE)
            sel = jnp.where(jidx == i, 1.0, 0.0)
            for r in range(kt):
                top = jnp.max(gm, axis=0, keepdims=True)
                first = jnp.min(jnp.where(gm == top, jidx, nb), axis=0, keepdims=True)
                pick = jidx == first
                sel = jnp.maximum(sel, jnp.where(pick & past, 1.0, 0.0) * jnp.where(r < i, 1.0, 0.0))
                gm = jnp.where(pick, -jnp.inf, gm)
            sel_ref[i, a] = sel
        return carry

    lax.fori_loop(0, nb, select, 0, unroll=2)

    scale = dh ** -0.5
    slopes = [slope_ref[na * hg + a] for a in range(na)]

    def block_ids(i, t):
        ids = []
        for u in range(nu):
            b = i - nu * t - u
            ids.append((jnp.maximum(b, 0), jnp.where(b >= 0, 1.0, 0.0)))
        return ids

    def scores(i, t):
        qt = qt_ref[i]
        qm = [jnp.where(head_rows[a], qt, jnp.zeros_like(qt)) * jnp.asarray(scale, BF16) for a in range(na)]
        s_all, stats = [], []
        for u, (bc, valid) in enumerate(block_ids(i, t)):
            k_b = k_ref[pl.ds(pl.multiple_of(bc * bs, bs), bs), :]
            diag = jnp.where(bc == i, 1, 0)
            for a in range(na):
                s = _dot(k_b, qm[a]) + pb_ref[diag, a]
                chosen = sel_ref[i, a, pl.ds(bc, 1), :] * valid > 0.5
                off = jnp.where(chosen, slopes[a] * (bc.astype(F32) * float(bs)), -jnp.inf)
                s_all.append(s)
                stats.append((jnp.max(s, axis=0, keepdims=True) + off, off))
        return s_all, tuple(stats)

    ones_rows = jnp.ones((V_AUG_ROWS - dh, nu * bs), BF16)

    def step(n, carry):
        stats, state = carry
        i, t = tab_ref[0, n], tab_ref[1, n]
        s_cur = [s_ref[ua] for ua in range(nu * na)]
        s_nxt, stats_nxt = scores(tab_ref[0, n + 1], tab_ref[1, n + 1])
        ids = block_ids(i, t)
        first = t == 0
        cap = jnp.where(first, -jnp.inf, jnp.inf)
        keep = jnp.where(first, 0.0, 1.0)
        out = []
        for a in range(na):
            m_old = jnp.minimum(state[a][0], cap)
            m_new = m_old
            for u in range(nu):
                m_new = jnp.maximum(m_new, stats[u * na + a][0])
            p = [jnp.exp((s_cur[u * na + a] - (m_new - stats[u * na + a][1])).astype(BF16)) for u in range(nu)]
            v = jnp.concatenate([vt_ref[ids[u][0], a * dh:(a + 1) * dh, :] for u in range(nu)], axis=1)
            pv = _dot(jnp.concatenate([v, ones_rows], axis=0), jnp.concatenate(p, axis=0))
            out.append((m_new, (jnp.exp(m_old - m_new) * keep) * state[a][1] + pv))
        for ua in range(nu * na):
            s_ref[ua] = s_nxt[ua]

        for a in range(na):
            acc_ref[i, a] = out[a][1]
        return stats_nxt, tuple(out)

    s0, stats0 = scores(tab_ref[0, 0], tab_ref[1, 0])
    for ua in range(nu * na):
        s_ref[ua] = s0[ua]
    init = tuple((jnp.full((1, bs), -jnp.inf, F32), jnp.zeros((V_AUG_ROWS, bs), F32)) for _ in range(na))
    lax.fori_loop(0, n_steps, step, (stats0, init), unroll=8)

    def finalize(i, carry):
        o = [acc_ref[i, a, 0:dh] / acc_ref[i, a, dh:dh + 1] for a in range(na)]
        o_ref[pl.ds(pl.multiple_of(i * bs, bs), bs), :] = jnp.concatenate(o, axis=0).T.astype(o_ref.dtype)
        return carry

    lax.fori_loop(0, nb, finalize, 0, unroll=2)


def _moba(qt, k, vt, batch, seq):
    n = k.shape[0]
    bs, dh, nh, na = MOBA_BLOCK, MOBA_HEAD_DIM, MOBA_HEADS, MOBA_HEADS_PER_STEP
    nb = seq // bs
    dim = nh * dh
    slopes = jnp.exp2(-8.0 * jnp.arange(1, nh + 1, dtype=F32) / nh)
    key_pos = jnp.arange(bs, dtype=F32)[:, None]
    pos_bias = slopes[:, None, None] * jnp.broadcast_to(key_pos, (bs, bs))[None]
    causal = jnp.arange(bs)[:, None] <= jnp.arange(bs)[None, :]
    pos_bias = jnp.stack([pos_bias, jnp.where(causal[None], pos_bias, -jnp.inf)])
    nu = MOBA_BLOCKS_PER_STEP
    table = [(i, t) for i in range(nb) for t in range((i + nu) // nu)] + [(0, 0)]
    table = jnp.asarray(np.array(table, np.int32).T)
    smem = pl.BlockSpec(memory_space=pltpu.SMEM)
    transposed = pl.BlockSpec((nb, na * dh, bs), lambda b, hg: (b, hg, 0))
    return pl.pallas_call(
        functools.partial(_moba_kernel, nb=nb, n_steps=table.shape[1] - 1),
        grid=(batch, nh // na),
        in_specs=[smem, smem, transposed,
                  pl.BlockSpec((seq, na * dh), lambda b, hg: (b, hg)),
                  transposed,
                  pl.BlockSpec((2, na, bs, bs), lambda b, hg: (0, hg, 0, 0))],
        out_specs=pl.BlockSpec((seq, na * dh), lambda b, hg: (b, hg)),
        out_shape=jax.ShapeDtypeStruct((n, dim), BF16),
        scratch_shapes=[pltpu.VMEM((nb, na * dh), F32), pltpu.VMEM((nb, na, nb, bs), F32),
                        pltpu.VMEM((nu * na, bs, bs), F32), pltpu.VMEM((nb, na, V_AUG_ROWS, bs), F32)],
        compiler_params=_params("parallel", "parallel"),
        name="moba",
    )(slopes, table, qt, k, vt, pos_bias)


def kernel(x, ffn1_norm, ffn1_w_gu, ffn1_w_down, mix_norm, w_in, hgrn_lb_logits, hgrn_norm,
           w_branch_a, w_branch_b, w_out, ffn2_norm, ffn2_w_gu, ffn2_w_down, final_norm):
    batch, seq, d = x.shape
    depth = ffn1_norm.shape[0]
    hdim = 4 * HGRN_HEADS * HGRN_DK
    mdim = MOBA_HEADS * MOBA_HEAD_DIM
    xf = x.reshape(batch * seq, d)
    w1gu, w1d, w2gu, w2d, wi, wa, wb, wo = (_to_bf16(w) for w in (
        ffn1_w_gu, ffn1_w_down, ffn2_w_gu, ffn2_w_down, w_in, w_branch_a, w_branch_b, w_out))
    for l in range(depth):
        xf = _ffn(xf, ffn1_norm[l], w1gu, w1d, l)
        w_q = wi[l, :, hdim:hdim + mdim]
        w_v = wi[l, :, hdim + 2 * mdim:hdim + 3 * mdim]
        h4, k, qt, vt, gates = _inproj(xf, mix_norm[l], wi, jnp.concatenate([w_q, w_v], axis=1).T, l)
        oa = _hgrn(h4, hgrn_lb_logits, hgrn_norm[l], l, batch, seq)
        ob = _moba(qt, k, vt, batch, seq)
        xf = _ffn(xf, ffn2_norm[l], w2gu, w2d, l, merge=(oa, ob, gates, wa, wb, wo),
                  final_w=final_norm if l == depth - 1 else None)
    return xf.reshape(batch, seq, d)
```

```python
import functools

import numpy as np
import jax
import jax.numpy as jnp
from jax import lax
from jax.experimental import pallas as pl
from jax.experimental.pallas import tpu as pltpu

F32 = jnp.float32
BF16 = jnp.bfloat16

NORM_EPS = 1e-6
MIN_FORGET = 1e-20
MASK_VALUE = -1e9

HGRN_HEADS = 4
HGRN_DK = 128
MOBA_HEADS = 8
MOBA_HEAD_DIM = 64
MOBA_BLOCK = 256
MOBA_TOPK = 3
MOBA_SCALE = MOBA_HEAD_DIM ** -0.5

V7X_VMEM_LIMIT_BYTES = 56 * 1024 * 1024

LOG2_E = 1.4426950408889634
HGRN_CHUNK = 128
HGRN_MATMUL_LEVEL_LIMIT = 8
HGRN_ROWS_PER_STEP = 1024
V_AUG_ROWS = 80
MOBA_HEADS_PER_STEP = 4
MOBA_BLOCKS_PER_STEP = 2


def _rmsnorm(x, w):
    return x * lax.rsqrt(jnp.mean(x * x, axis=-1, keepdims=True) + NORM_EPS) * w


def _silu(x):
    return x * jax.nn.sigmoid(x)


def _dot(a, b):
    return jnp.dot(a, b, preferred_element_type=F32)


def _dot_nt(a, b):
    return lax.dot_general(a, b, (((1,), (1,)), ((), ())), preferred_element_type=F32)


def _resident(shape):
    nd = len(shape)
    return pl.BlockSpec(shape, lambda *_: (0,) * nd, pipeline_mode=pl.Buffered(1))


def _resident_layer(stacked, layer):
    return pl.BlockSpec((None,) + stacked.shape[1:], lambda *_: (layer, 0, 0), pipeline_mode=pl.Buffered(1))


def _cast_kernel(w_ref, o_ref):
    o_ref[...] = w_ref[...].astype(o_ref.dtype)


def _to_bf16(w):
    depth, r, c = w.shape
    rows = max(k for k in range(16, r + 1, 16) if r % k == 0 and k * c * 4 <= 6 * 1024 * 1024)
    block = pl.BlockSpec((1, rows, c), lambda l, i: (l, i, 0))
    return pl.pallas_call(
        _cast_kernel,
        grid=(depth, r // rows),
        in_specs=[block],
        out_specs=block,
        out_shape=jax.ShapeDtypeStruct(w.shape, BF16),
        compiler_params=_params("parallel", "parallel"),
        name="to_bf16",
    )(w)


def _params(*sem):
    return pltpu.CompilerParams(dimension_semantics=sem, vmem_limit_bytes=V7X_VMEM_LIMIT_BYTES)


def _ffn_kernel(x_ref, *refs, merge, final):
    refs = list(refs)
    o_ref = refs.pop()
    parts = 1 if merge else 2
    rows = x_ref.shape[0] // parts
    for rs in [slice(j * rows, (j + 1) * rows) for j in range(parts)]:
        rest = refs
        x = x_ref[rs, :]
        if merge:
            oa_ref, ob_ref, g_ref, wa_ref, wb_ref, wo_ref = rest[:6]
            rest = rest[6:]
            d = x.shape[1]
            ga = jax.nn.sigmoid(g_ref[rs, 0:d].astype(F32))
            gb = jax.nn.sigmoid(g_ref[rs, d:2 * d].astype(F32))
            merged = ga * _dot(oa_ref[rs, :], wa_ref[...]) + gb * _dot(ob_ref[rs, :], wb_ref[...])
            x = x + _dot(merged.astype(BF16), wo_ref[...])
        nw_ref, wgu_ref, wd_ref = rest[:3]
        ff = wd_ref.shape[0]
        h = _rmsnorm(x, nw_ref[...]).astype(BF16)
        g = _dot(h, wgu_ref[:, :ff])
        u = _dot(h, wgu_ref[:, ff:])
        a = (_silu(g) * u).astype(BF16)
        y = x + 0.5 * _dot(a, wd_ref[...])
        if final:
            y = _rmsnorm(y, rest[3][...])
        o_ref[rs, :] = y


def _ffn(x, nw, wgu, wd, layer, merge=None, final_w=None, tm=512):
    n, d = x.shape

    def row(width):
        return pl.BlockSpec((tm, width), lambda i: (i, 0))

    in_specs, args = [row(d)], [x]
    if merge is not None:
        in_specs += [row(a.shape[1]) for a in merge[:3]] + [_resident_layer(a, layer) for a in merge[3:]]
        args += list(merge)
    in_specs += [_resident((1, d)), _resident_layer(wgu, layer), _resident_layer(wd, layer)]
    args += [nw.reshape(1, d), wgu, wd]
    if final_w is not None:
        in_specs.append(_resident((1, d)))
        args.append(final_w.reshape(1, d))
    return pl.pallas_call(
        functools.partial(_ffn_kernel, merge=merge is not None, final=final_w is not None),
        grid=(n // tm,),
        in_specs=in_specs,
        out_specs=row(d),
        out_shape=jax.ShapeDtypeStruct((n, d), F32),
        compiler_params=_params("parallel"),
        name="ffn",
    )(*args)


def _inproj_kernel(x_ref, nw_ref, wi_ref, wqvt_ref, h_ref, k_ref, qt_ref, vt_ref, g_ref):
    dh, dk, dg = h_ref.shape[1], k_ref.shape[1], g_ref.shape[1]
    k0 = dh + dk
    dq = qt_ref.shape[1]
    for blk in range(qt_ref.shape[0]):
        rs = slice(blk * MOBA_BLOCK, (blk + 1) * MOBA_BLOCK)
        h = _rmsnorm(x_ref[rs, :], nw_ref[...]).astype(BF16)
        h_ref[rs, :] = _dot(h, wi_ref[:, :dh])
        k_ref[rs, :] = _dot(h, wi_ref[:, k0:k0 + dk]).astype(BF16)
        qvt = _dot_nt(wqvt_ref[...], h)
        qt_ref[blk] = (qvt[:dq] * MOBA_SCALE).astype(BF16)
        vt_ref[blk] = qvt[dq:].astype(BF16)
        g_ref[rs, :] = _dot(h, wi_ref[:, wi_ref.shape[1] - dg:]).astype(BF16)


def _inproj(x, nw, wi, wqvt, layer, tm=512):
    n, d = x.shape
    dqv = wqvt.shape[0]
    dk = dqv // 2
    dh = 4 * HGRN_HEADS * HGRN_DK
    dg = wi.shape[2] - dh - 3 * dk
    blocks = tm // MOBA_BLOCK
    transposed = pl.BlockSpec((blocks, dqv // 2, MOBA_BLOCK), lambda i: (i, 0, 0))
    return pl.pallas_call(
        _inproj_kernel,
        grid=(n // tm,),
        in_specs=[pl.BlockSpec((tm, d), lambda i: (i, 0)), _resident((1, d)),
                  _resident_layer(wi, layer), _resident((dqv, d))],
        out_specs=[pl.BlockSpec((tm, dh), lambda i: (i, 0)),
                   pl.BlockSpec((tm, dk), lambda i: (i, 0)),
                   transposed, transposed,
                   pl.BlockSpec((tm, dg), lambda i: (i, 0))],
        out_shape=[jax.ShapeDtypeStruct((n, dh), F32),
                   jax.ShapeDtypeStruct((n, dk), BF16),
                   jax.ShapeDtypeStruct((n // MOBA_BLOCK, dqv // 2, MOBA_BLOCK), BF16),
                   jax.ShapeDtypeStruct((n // MOBA_BLOCK, dqv // 2, MOBA_BLOCK), BF16),
                   jax.ShapeDtypeStruct((n, dg), BF16)],
        compiler_params=_params("parallel"),
        name="inproj",
    )(x, nw.reshape(1, d), wi, wqvt)


def _hgrn_constants(c):
    t = np.arange(c)
    sp = t[None, :]
    tt = t[:, None]
    rows = [sp <= tt, sp > tt]
    masks, upper = [], []
    m = 1
    while m < c:
        up = (t // m) % 2 == 1
        bnd = ((t // (2 * m)) * 2 * m + m - 1)[:, None]
        if m < HGRN_MATMUL_LEVEL_LIMIT:
            rows.append(np.where(up[:, None], (sp > bnd) & (sp <= tt), (sp > tt) & (sp <= bnd)))
        same = (tt // (2 * m)) == (sp // (2 * m))
        masks.append(up[:, None] & (~up)[None, :] & same)
        upper.append(np.broadcast_to(up[:, None], (c, HGRN_DK)))
        m *= 2
    w = np.concatenate(rows, axis=0).astype(np.float32)
    w = np.concatenate([w, w], axis=1)
    return (jnp.asarray(w, BF16), jnp.asarray(np.stack(masks), F32),
            jnp.asarray(np.stack(upper), F32))


def _hgrn_kernel(lbl_ref, nw_ref, w_ref, lm_ref, up_ref, h_ref, o_ref, st_ref, *, layer, n_levels):
    c = HGRN_CHUNK
    dk = HGRN_DK
    kdim = HGRN_HEADS * dk

    @pl.when(pl.program_id(1) == 0)
    def _():
        st_ref[...] = jnp.zeros_like(st_ref)

    logits = lbl_ref[...]
    e = jnp.exp(logits - jnp.max(logits, axis=0, keepdims=True))
    sm = e / jnp.sum(e, axis=0, keepdims=True)
    cum = sm[0:1, :]
    for j in range(1, layer + 1):
        cum = cum + sm[j:j + 1, :]
    lb_all = jnp.clip(cum - sm[0:1, :], 0.0, 1.0)
    w = w_ref[...]
    nw = nw_ref[...]

    heads = [slice(h * dk, (h + 1) * dk) for h in range(HGRN_HEADS)]

    def chunk(ci, carry):
        r0 = pl.multiple_of(ci * c, c)
        q = _silu(h_ref[pl.ds(r0, c), 0:kdim])
        sig = jax.nn.sigmoid(h_ref[pl.ds(r0, c), kdim:2 * kdim])
        inp = h_ref[pl.ds(r0, c), 2 * kdim:3 * kdim]
        og = h_ref[pl.ds(r0, c), 3 * kdim:4 * kdim]
        f = lb_all + (1.0 - lb_all) * sig
        g = jnp.log(jnp.maximum(f, MIN_FORGET)) * LOG2_E
        k = (1.0 - lb_all) * (1.0 - sig)

        g_hi = g.astype(BF16)
        g_lo = (g - g_hi.astype(F32)).astype(BF16)
        e = _dot(w, jnp.concatenate([g_hi, g_lo], axis=0))
        x = jnp.exp2(e)
        b = e[0:c]

        qd = (q * x[0:c]).astype(BF16)
        kd = (k * x[c:2 * c]).astype(BF16)
        decay_all = x[c - 1:c, :]
        dqk = q - k
        a = [jnp.zeros((c, c), F32) for _ in heads]
        for l in range(n_levels):
            m = 2 ** l
            if m < HGRN_MATMUL_LEVEL_LIMIT:
                up = jnp.concatenate([up_ref[l]] * HGRN_HEADS, axis=1)
                z = ((k + up * dqk) * x[(2 + l) * c:(3 + l) * c]).astype(BF16)
            else:
                b3 = b.reshape(c // (2 * m), 2 * m, kdim)
                d = (b3 - b3[:, m - 1:m, :]).reshape(c, kdim)
                blocks = [slice(j * m, (j + 1) * m) for j in range(c // m)]
                e_l = jnp.concatenate([d[rs] if j % 2 else -d[rs] for j, rs in enumerate(blocks)], axis=0)
                qk_rows = jnp.concatenate([q[rs] if j % 2 else k[rs] for j, rs in enumerate(blocks)], axis=0)
                z = (qk_rows * jnp.exp2(e_l)).astype(BF16)
            mask = lm_ref[l] > 0.5
            a = [jnp.where(mask, _dot_nt(z[:, hs], z[:, hs]), a[h]) for h, hs in enumerate(heads)]
        inp_b = inp.astype(BF16)
        qk = q * k
        outs = []
        for h, hs in enumerate(heads):
            st = st_ref[h]
            o = (_dot(a[h].astype(BF16), inp_b[:, hs]) + _dot_nt(qd[:, hs], st.astype(BF16))
                 + jnp.sum(qk[:, hs], axis=-1, keepdims=True) * inp[:, hs])
            st_ref[h] = st * decay_all[:, hs] + _dot(inp[:, hs].T.astype(BF16), kd[:, hs])
            outs.append(o * lax.rsqrt(jnp.mean(o * o, axis=-1, keepdims=True) + NORM_EPS) * nw)
        o_ref[pl.ds(r0, c), :] = (jnp.concatenate(outs, axis=1) * _silu(og)).astype(o_ref.dtype)
        return carry

    lax.fori_loop(0, h_ref.shape[0] // c, chunk, 0, unroll=True)


def _hgrn(h4, lb_logits, norm_w, layer, batch, seq):
    n = h4.shape[0]
    c = HGRN_CHUNK
    tb = HGRN_ROWS_PER_STEP
    steps = seq // tb
    w, lm, up = _hgrn_constants(c)
    n_levels = lm.shape[0]
    vdim = HGRN_HEADS * HGRN_DK
    return pl.pallas_call(
        functools.partial(_hgrn_kernel, layer=layer, n_levels=n_levels),
        grid=(batch, steps),
        in_specs=[_resident(lb_logits.shape), _resident((1, HGRN_DK)), _resident(w.shape),
                  _resident(lm.shape), _resident(up.shape),
                  pl.BlockSpec((tb, h4.shape[1]), lambda b, s: (b * steps + s, 0))],
        out_specs=pl.BlockSpec((tb, vdim), lambda b, s: (b * steps + s, 0)),
        out_shape=jax.ShapeDtypeStruct((n, vdim), BF16),
        scratch_shapes=[pltpu.VMEM((HGRN_HEADS, HGRN_DK, HGRN_DK), F32)],
        compiler_params=_params("parallel", "arbitrary"),
        name="hgrn",
    )(lb_logits, norm_w.reshape(1, HGRN_DK), w, lm, up, h4)


def _moba_kernel(slope_ref, tab_ref, qt_ref, k_ref, vt_ref, pb_ref, o_ref, km_ref, sel_ref, qm_ref, s_ref, acc_ref, *, nb, n_steps):
    bs = MOBA_BLOCK
    dh = MOBA_HEAD_DIM
    na = MOBA_HEADS_PER_STEP
    nu = MOBA_BLOCKS_PER_STEP
    width = na * dh
    hg = pl.program_id(1)

    for j in range(nb):
        km_ref[j:j + 1, :] = jnp.mean(k_ref[j * bs:(j + 1) * bs, :].astype(F32), axis=0, keepdims=True)

    lane = lax.broadcasted_iota(jnp.int32, (1, width), 1)
    row = lax.broadcasted_iota(jnp.int32, (width, 1), 0)
    head_lanes = [(lane >= a * dh) & (lane < (a + 1) * dh) for a in range(na)]
    head_rows = [(row >= a * dh) & (row < (a + 1) * dh) for a in range(na)]

    km = km_ref[...]
    km2 = jnp.concatenate([jnp.where(head_lanes[a], km, 0.0) for a in range(na)], axis=0)
    p0 = km2.astype(BF16)
    r1 = km2 - p0.astype(F32)
    p1 = r1.astype(BF16)
    p2 = (r1 - p1.astype(F32)).astype(BF16)
    jidx = lax.broadcasted_iota(jnp.int32, (nb, bs), 0)
    kt = min(MOBA_TOPK, nb)

    def select(i, carry):
        qt = qt_ref[i]
        gate2 = _dot(p0, qt) + _dot(p1, qt) + _dot(p2, qt)
        past = jidx < i
        for a in range(na):
            qm_ref[i, a] = jnp.where(head_rows[a], qt, jnp.zeros_like(qt))
            gm = jnp.where(past, gate2[a * nb:(a + 1) * nb, :], MASK_VALUE * MOBA_SCALE)
            sel = jnp.where(jidx == i, 1.0, 0.0)
            for r in range(kt):
                top = jnp.max(gm, axis=0, keepdims=True)
                first = jnp.min(jnp.where(gm == top, jidx, nb), axis=0, keepdims=True)
                pick = jidx == first
                sel = jnp.maximum(sel, jnp.where(pick & past, 1.0, 0.0) * jnp.where(r < i, 1.0, 0.0))
                gm = jnp.where(pick, -jnp.inf, gm)
            sel_ref[i, a] = sel
        return carry

    lax.fori_loop(0, nb, select, 0, unroll=2)

    slopes = [slope_ref[na * hg + a] for a in range(na)]

    def block_ids(i, t):
        ids = []
        for u in range(nu):
            b = i - nu * t - u
            ids.append((jnp.maximum(b, 0), jnp.where(b >= 0, 1.0, 0.0)))
        return ids

    def scores(i, t):
        ids = block_ids(i, t)
        keys = jnp.concatenate([k_ref[pl.ds(pl.multiple_of(bc * bs, bs), bs), :] for bc, _ in ids], axis=0)
        qk = [_dot(keys, qm_ref[i, a]) for a in range(na)]
        s_all, stats = [], []
        for u, (bc, valid) in enumerate(ids):
            diag = jnp.where(bc == i, 1, 0)
            for a in range(na):
                s = qk[a][u * bs:(u + 1) * bs] + pb_ref[diag, a]
                chosen = sel_ref[i, a, pl.ds(bc, 1), :] * valid > 0.5
                off = jnp.where(chosen, slopes[a] * (bc.astype(F32) * float(bs)), -jnp.inf)
                s_all.append(s)
                stats.append((jnp.max(s, axis=0, keepdims=True) + off, off))
        return s_all, tuple(stats)

    ones_rows = jnp.ones((V_AUG_ROWS - dh, nu * bs), BF16)

    def step(n, carry):
        stats, state = carry
        i, t = tab_ref[0, n], tab_ref[1, n]
        s_cur = [s_ref[ua] for ua in range(nu * na)]
        s_nxt, stats_nxt = scores(tab_ref[0, n + 1], tab_ref[1, n + 1])
        ids = block_ids(i, t)
        first = t == 0
        cap = jnp.where(first, -jnp.inf, jnp.inf)
        keep = jnp.where(first, 0.0, 1.0)
        out = []
        for a in range(na):
            m_old = jnp.minimum(state[a][0], cap)
            m_new = m_old
            for u in range(nu):
                m_new = jnp.maximum(m_new, stats[u * na + a][0])
            p = [jnp.exp((s_cur[u * na + a] - (m_new - stats[u * na + a][1])).astype(BF16)) for u in range(nu)]
            v = jnp.concatenate([vt_ref[ids[u][0], a * dh:(a + 1) * dh, :] for u in range(nu)], axis=1)
            pv = _dot(jnp.concatenate([v, ones_rows], axis=0), jnp.concatenate(p, axis=0))
            out.append((m_new, (jnp.exp(m_old - m_new) * keep) * state[a][1] + pv))
        for ua in range(nu * na):
            s_ref[ua] = s_nxt[ua]

        for a in range(na):
            acc_ref[i, a] = out[a][1]
        return stats_nxt, tuple(out)

    s0, stats0 = scores(tab_ref[0, 0], tab_ref[1, 0])
    for ua in range(nu * na):
        s_ref[ua] = s0[ua]
    init = tuple((jnp.full((1, bs), -jnp.inf, F32), jnp.zeros((V_AUG_ROWS, bs), F32)) for _ in range(na))
    lax.fori_loop(0, n_steps, step, (stats0, init), unroll=8)

    def finalize(i, carry):
        o = [acc_ref[i, a, 0:dh] / acc_ref[i, a, dh:dh + 1] for a in range(na)]
        o_ref[pl.ds(pl.multiple_of(i * bs, bs), bs), :] = jnp.concatenate(o, axis=0).T.astype(o_ref.dtype)
        return carry

    lax.fori_loop(0, nb, finalize, 0, unroll=2)


def _moba(qt, k, vt, batch, seq):
    n = k.shape[0]
    bs, dh, nh, na = MOBA_BLOCK, MOBA_HEAD_DIM, MOBA_HEADS, MOBA_HEADS_PER_STEP
    nb = seq // bs
    dim = nh * dh
    slopes = jnp.exp2(-8.0 * jnp.arange(1, nh + 1, dtype=F32) / nh)
    key_pos = jnp.arange(bs, dtype=F32)[:, None]
    pos_bias = slopes[:, None, None] * jnp.broadcast_to(key_pos, (bs, bs))[None]
    causal = jnp.arange(bs)[:, None] <= jnp.arange(bs)[None, :]
    pos_bias = jnp.stack([pos_bias, jnp.where(causal[None], pos_bias, -jnp.inf)])
    nu = MOBA_BLOCKS_PER_STEP
    table = [(i, t) for i in range(nb) for t in range((i + nu) // nu)] + [(0, 0)]
    table = jnp.asarray(np.array(table, np.int32).T)
    smem = pl.BlockSpec(memory_space=pltpu.SMEM)
    transposed = pl.BlockSpec((nb, na * dh, bs), lambda b, hg: (b, hg, 0))
    return pl.pallas_call(
        functools.partial(_moba_kernel, nb=nb, n_steps=table.shape[1] - 1),
        grid=(batch, nh // na),
        in_specs=[smem, smem, transposed,
                  pl.BlockSpec((seq, na * dh), lambda b, hg: (b, hg)),
                  transposed,
                  pl.BlockSpec((2, na, bs, bs), lambda b, hg: (0, hg, 0, 0))],
        out_specs=pl.BlockSpec((seq, na * dh), lambda b, hg: (b, hg)),
        out_shape=jax.ShapeDtypeStruct((n, dim), BF16),
        scratch_shapes=[pltpu.VMEM((nb, na * dh), F32), pltpu.VMEM((nb, na, nb, bs), F32),
                        pltpu.VMEM((nb, na, na * dh, bs), BF16),
                        pltpu.VMEM((nu * na, bs, bs), F32), pltpu.VMEM((nb, na, V_AUG_ROWS, bs), F32)],
        compiler_params=_params("parallel", "parallel"),
        name="moba",
    )(slopes, table, qt, k, vt, pos_bias)


def kernel(x, ffn1_norm, ffn1_w_gu, ffn1_w_down, mix_norm, w_in, hgrn_lb_logits, hgrn_norm,
           w_branch_a, w_branch_b, w_out, ffn2_norm, ffn2_w_gu, ffn2_w_down, final_norm):
    batch, seq, d = x.shape
    depth = ffn1_norm.shape[0]
    hdim = 4 * HGRN_HEADS * HGRN_DK
    mdim = MOBA_HEADS * MOBA_HEAD_DIM
    xf = x.reshape(batch * seq, d)
    w1gu, w1d, w2gu, w2d, wi, wa, wb, wo = (_to_bf16(w) for w in (
        ffn1_w_gu, ffn1_w_down, ffn2_w_gu, ffn2_w_down, w_in, w_branch_a, w_branch_b, w_out))
    for l in range(depth):
        xf = _ffn(xf, ffn1_norm[l], w1gu, w1d, l)
        w_q = wi[l, :, hdim:hdim + mdim]
        w_v = wi[l, :, hdim + 2 * mdim:hdim + 3 * mdim]
        h4, k, qt, vt, gates = _inproj(xf, mix_norm[l], wi, jnp.concatenate([w_q, w_v], axis=1).T, l)
        oa = _hgrn(h4, hgrn_lb_logits, hgrn_norm[l], l, batch, seq)
        ob = _moba(qt, k, vt, batch, seq)
        xf = _ffn(xf, ffn2_norm[l], w2gu, w2d, l, merge=(oa, ob, gates, wa, wb, wo),
                  final_w=final_norm if l == depth - 1 else None)
    return xf.reshape(batch, seq, d)
```

```python
import functools

import numpy as np
import jax
import jax.numpy as jnp
from jax import lax
from jax.experimental import pallas as pl
from jax.experimental.pallas import tpu as pltpu

F32 = jnp.float32
BF16 = jnp.bfloat16

NORM_EPS = 1e-6
MIN_FORGET = 1e-20
MASK_VALUE = -1e9

HGRN_HEADS = 4
HGRN_DK = 128
MOBA_HEADS = 8
MOBA_HEAD_DIM = 64
MOBA_BLOCK = 256
MOBA_TOPK = 3
MOBA_SCALE = MOBA_HEAD_DIM ** -0.5

V7X_VMEM_LIMIT_BYTES = 56 * 1024 * 1024

LOG2_E = 1.4426950408889634
HGRN_CHUNK = 128
HGRN_MATMUL_LEVEL_LIMIT = 8
HGRN_ROWS_PER_STEP = 1024
V_AUG_ROWS = 80
MOBA_HEADS_PER_STEP = 4
MOBA_BLOCKS_PER_STEP = 2


def _rmsnorm(x, w):
    return x * lax.rsqrt(jnp.mean(x * x, axis=-1, keepdims=True) + NORM_EPS) * w


def _silu(x):
    return x * jax.nn.sigmoid(x)


def _dot(a, b):
    return jnp.dot(a, b, preferred_element_type=F32)


def _dot_nt(a, b):
    return lax.dot_general(a, b, (((1,), (1,)), ((), ())), preferred_element_type=F32)


def _resident(shape):
    nd = len(shape)
    return pl.BlockSpec(shape, lambda *_: (0,) * nd, pipeline_mode=pl.Buffered(1))


def _resident_layer(stacked, layer):
    return pl.BlockSpec((None,) + stacked.shape[1:], lambda *_: (layer, 0, 0), pipeline_mode=pl.Buffered(1))


def _cast_kernel(w_ref, o_ref):
    o_ref[...] = w_ref[...].astype(o_ref.dtype)


def _to_bf16(w):
    depth, r, c = w.shape
    rows = max(k for k in range(16, r + 1, 16) if r % k == 0 and k * c * 4 <= 6 * 1024 * 1024)
    block = pl.BlockSpec((1, rows, c), lambda l, i: (l, i, 0))
    return pl.pallas_call(
        _cast_kernel,
        grid=(depth, r // rows),
        in_specs=[block],
        out_specs=block,
        out_shape=jax.ShapeDtypeStruct(w.shape, BF16),
        compiler_params=_params("parallel", "parallel"),
        name="to_bf16",
    )(w)


def _params(*sem):
    return pltpu.CompilerParams(dimension_semantics=sem, vmem_limit_bytes=V7X_VMEM_LIMIT_BYTES)


def _ffn_kernel(x_ref, *refs, merge, final):
    refs = list(refs)
    o_ref = refs.pop()
    parts = 1 if merge else 2
    rows = x_ref.shape[0] // parts
    for rs in [slice(j * rows, (j + 1) * rows) for j in range(parts)]:
        rest = refs
        x = x_ref[rs, :]
        if merge:
            oa_ref, ob_ref, g_ref, wa_ref, wb_ref, wo_ref = rest[:6]
            rest = rest[6:]
            d = x.shape[1]
            ga = jax.nn.sigmoid(g_ref[rs, 0:d].astype(F32))
            gb = jax.nn.sigmoid(g_ref[rs, d:2 * d].astype(F32))
            merged = ga * _dot(oa_ref[rs, :], wa_ref[...]) + gb * _dot(ob_ref[rs, :], wb_ref[...])
            x = x + _dot(merged.astype(BF16), wo_ref[...])
        nw_ref, wgu_ref, wd_ref = rest[:3]
        ff = wd_ref.shape[0]
        h = _rmsnorm(x, nw_ref[...]).astype(BF16)
        g = _dot(h, wgu_ref[:, :ff])
        u = _dot(h, wgu_ref[:, ff:])
        a = (_silu(g) * u).astype(BF16)
        y = x + 0.5 * _dot(a, wd_ref[...])
        if final:
            y = _rmsnorm(y, rest[3][...])
        o_ref[rs, :] = y


def _ffn(x, nw, wgu, wd, layer, merge=None, final_w=None, tm=512):
    n, d = x.shape

    def row(width):
        return pl.BlockSpec((tm, width), lambda i: (i, 0))

    in_specs, args = [row(d)], [x]
    if merge is not None:
        in_specs += [row(a.shape[1]) for a in merge[:3]] + [_resident_layer(a, layer) for a in merge[3:]]
        args += list(merge)
    in_specs += [_resident((1, d)), _resident_layer(wgu, layer), _resident_layer(wd, layer)]
    args += [nw.reshape(1, d), wgu, wd]
    if final_w is not None:
        in_specs.append(_resident((1, d)))
        args.append(final_w.reshape(1, d))
    return pl.pallas_call(
        functools.partial(_ffn_kernel, merge=merge is not None, final=final_w is not None),
        grid=(n // tm,),
        in_specs=in_specs,
        out_specs=row(d),
        out_shape=jax.ShapeDtypeStruct((n, d), F32),
        compiler_params=_params("parallel"),
        name="ffn",
    )(*args)


def _inproj_kernel(x_ref, nw_ref, wi_ref, wqvt_ref, h_ref, k_ref, qt_ref, vt_ref, g_ref):
    dh, dk, dg = h_ref.shape[1], k_ref.shape[1], g_ref.shape[1]
    k0 = dh + dk
    dq = qt_ref.shape[1]
    for blk in range(qt_ref.shape[0]):
        rs = slice(blk * MOBA_BLOCK, (blk + 1) * MOBA_BLOCK)
        h = _rmsnorm(x_ref[rs, :], nw_ref[...]).astype(BF16)
        h_ref[rs, :] = _dot(h, wi_ref[:, :dh])
        k_ref[rs, :] = _dot(h, wi_ref[:, k0:k0 + dk]).astype(BF16)
        qvt = _dot_nt(wqvt_ref[...], h)
        qt_ref[blk] = (qvt[:dq] * MOBA_SCALE).astype(BF16)
        vt_ref[blk] = qvt[dq:].astype(BF16)
        g_ref[rs, :] = _dot(h, wi_ref[:, wi_ref.shape[1] - dg:]).astype(BF16)


def _inproj(x, nw, wi, wqvt, layer, tm=512):
    n, d = x.shape
    dqv = wqvt.shape[0]
    dk = dqv // 2
    dh = 4 * HGRN_HEADS * HGRN_DK
    dg = wi.shape[2] - dh - 3 * dk
    blocks = tm // MOBA_BLOCK
    transposed = pl.BlockSpec((blocks, dqv // 2, MOBA_BLOCK), lambda i: (i, 0, 0))
    return pl.pallas_call(
        _inproj_kernel,
        grid=(n // tm,),
        in_specs=[pl.BlockSpec((tm, d), lambda i: (i, 0)), _resident((1, d)),
                  _resident_layer(wi, layer), _resident((dqv, d))],
        out_specs=[pl.BlockSpec((tm, dh), lambda i: (i, 0)),
                   pl.BlockSpec((tm, dk), lambda i: (i, 0)),
                   transposed, transposed,
                   pl.BlockSpec((tm, dg), lambda i: (i, 0))],
        out_shape=[jax.ShapeDtypeStruct((n, dh), F32),
                   jax.ShapeDtypeStruct((n, dk), BF16),
                   jax.ShapeDtypeStruct((n // MOBA_BLOCK, dqv // 2, MOBA_BLOCK), BF16),
                   jax.ShapeDtypeStruct((n // MOBA_BLOCK, dqv // 2, MOBA_BLOCK), BF16),
                   jax.ShapeDtypeStruct((n, dg), BF16)],
        compiler_params=_params("parallel"),
        name="inproj",
    )(x, nw.reshape(1, d), wi, wqvt)


def _hgrn_constants(c):
    t = np.arange(c)
    sp = t[None, :]
    tt = t[:, None]
    rows = [sp <= tt, sp > tt]
    masks, upper = [], []
    m = 1
    while m < c:
        up = (t // m) % 2 == 1
        bnd = ((t // (2 * m)) * 2 * m + m - 1)[:, None]
        if m < HGRN_MATMUL_LEVEL_LIMIT:
            rows.append(np.where(up[:, None], (sp > bnd) & (sp <= tt), (sp > tt) & (sp <= bnd)))
        same = (tt // (2 * m)) == (sp // (2 * m))
        masks.append(up[:, None] & (~up)[None, :] & same)
        upper.append(np.broadcast_to(up[:, None], (c, HGRN_DK)))
        m *= 2
    w = np.concatenate(rows, axis=0).astype(np.float32)
    w = np.concatenate([w, w], axis=1)
    return (jnp.asarray(w, BF16), jnp.asarray(np.stack(masks), F32),
            jnp.asarray(np.stack(upper), F32))


def _hgrn_kernel(lbl_ref, nw_ref, w_ref, lm_ref, up_ref, h_ref, o_ref, st_ref, *, layer, n_levels):
    c = HGRN_CHUNK
    dk = HGRN_DK
    kdim = HGRN_HEADS * dk

    @pl.when(pl.program_id(1) == 0)
    def _():
        st_ref[...] = jnp.zeros_like(st_ref)

    logits = lbl_ref[...]
    e = jnp.exp(logits - jnp.max(logits, axis=0, keepdims=True))
    sm = e / jnp.sum(e, axis=0, keepdims=True)
    cum = sm[0:1, :]
    for j in range(1, layer + 1):
        cum = cum + sm[j:j + 1, :]
    lb_all = jnp.clip(cum - sm[0:1, :], 0.0, 1.0)
    w = w_ref[...]
    nw = nw_ref[...]

    heads = [slice(h * dk, (h + 1) * dk) for h in range(HGRN_HEADS)]

    def chunk(ci, carry):
        r0 = pl.multiple_of(ci * c, c)
        q = _silu(h_ref[pl.ds(r0, c), 0:kdim])
        sig = jax.nn.sigmoid(h_ref[pl.ds(r0, c), kdim:2 * kdim])
        inp = h_ref[pl.ds(r0, c), 2 * kdim:3 * kdim]
        og = h_ref[pl.ds(r0, c), 3 * kdim:4 * kdim]
        f = lb_all + (1.0 - lb_all) * sig
        g = jnp.log(jnp.maximum(f, MIN_FORGET)) * LOG2_E
        k = (1.0 - lb_all) * (1.0 - sig)

        g_hi = g.astype(BF16)
        g_lo = (g - g_hi.astype(F32)).astype(BF16)
        e = _dot(w, jnp.concatenate([g_hi, g_lo], axis=0))
        x = jnp.exp2(e)
        b = e[0:c]

        qd = (q * x[0:c]).astype(BF16)
        kd = (k * x[c:2 * c]).astype(BF16)
        decay_all = x[c - 1:c, :]
        dqk = q - k
        a = [jnp.zeros((c, c), F32) for _ in heads]
        for l in range(n_levels):
            m = 2 ** l
            if m < HGRN_MATMUL_LEVEL_LIMIT:
                up = jnp.concatenate([up_ref[l]] * HGRN_HEADS, axis=1)
                z = ((k + up * dqk) * x[(2 + l) * c:(3 + l) * c]).astype(BF16)
            else:
                b3 = b.reshape(c // (2 * m), 2 * m, kdim)
                d = (b3 - b3[:, m - 1:m, :]).reshape(c, kdim)
                blocks = [slice(j * m, (j + 1) * m) for j in range(c // m)]
                e_l = jnp.concatenate([d[rs] if j % 2 else -d[rs] for j, rs in enumerate(blocks)], axis=0)
                qk_rows = jnp.concatenate([q[rs] if j % 2 else k[rs] for j, rs in enumerate(blocks)], axis=0)
                z = (qk_rows * jnp.exp2(e_l)).astype(BF16)
            mask = lm_ref[l] > 0.5
            a = [jnp.where(mask, _dot_nt(z[:, hs], z[:, hs]), a[h]) for h, hs in enumerate(heads)]
        inp_b = inp.astype(BF16)
        qk = q * k
        outs = []
        for h, hs in enumerate(heads):
            st = st_ref[h]
            o = (_dot(a[h].astype(BF16), inp_b[:, hs]) + _dot_nt(qd[:, hs], st.astype(BF16))
                 + jnp.sum(qk[:, hs], axis=-1, keepdims=True) * inp[:, hs])
            st_ref[h] = st * decay_all[:, hs] + _dot(inp[:, hs].T.astype(BF16), kd[:, hs])
            outs.append(o * lax.rsqrt(jnp.mean(o * o, axis=-1, keepdims=True) + NORM_EPS) * nw)
        o_ref[pl.ds(r0, c), :] = (jnp.concatenate(outs, axis=1) * _silu(og)).astype(o_ref.dtype)
        return carry

    lax.fori_loop(0, h_ref.shape[0] // c, chunk, 0, unroll=True)


def _hgrn(h4, lb_logits, norm_w, layer, batch, seq):
    n = h4.shape[0]
    c = HGRN_CHUNK
    tb = HGRN_ROWS_PER_STEP
    steps = seq // tb
    w, lm, up = _hgrn_constants(c)
    n_levels = lm.shape[0]
    vdim = HGRN_HEADS * HGRN_DK
    return pl.pallas_call(
        functools.partial(_hgrn_kernel, layer=layer, n_levels=n_levels),
        grid=(batch, steps),
        in_specs=[_resident(lb_logits.shape), _resident((1, HGRN_DK)), _resident(w.shape),
                  _resident(lm.shape), _resident(up.shape),
                  pl.BlockSpec((tb, h4.shape[1]), lambda b, s: (b * steps + s, 0))],
        out_specs=pl.BlockSpec((tb, vdim), lambda b, s: (b * steps + s, 0)),
        out_shape=jax.ShapeDtypeStruct((n, vdim), BF16),
        scratch_shapes=[pltpu.VMEM((HGRN_HEADS, HGRN_DK, HGRN_DK), F32)],
        compiler_params=_params("parallel", "arbitrary"),
        name="hgrn",
    )(lb_logits, norm_w.reshape(1, HGRN_DK), w, lm, up, h4)


def _moba_kernel(slope_ref, tab_ref, qt_ref, k_ref, vt_ref, pb_ref, o_ref, km_ref, sel_ref, qm_ref, s_ref, acc_ref, *, nb, n_steps):
    bs = MOBA_BLOCK
    dh = MOBA_HEAD_DIM
    na = MOBA_HEADS_PER_STEP
    nu = MOBA_BLOCKS_PER_STEP
    width = na * dh
    hg = pl.program_id(1)

    for j in range(nb):
        km_ref[j:j + 1, :] = jnp.mean(k_ref[j * bs:(j + 1) * bs, :].astype(F32), axis=0, keepdims=True)

    lane = lax.broadcasted_iota(jnp.int32, (1, width), 1)
    row = lax.broadcasted_iota(jnp.int32, (width, 1), 0)
    head_lanes = [(lane >= a * dh) & (lane < (a + 1) * dh) for a in range(na)]
    head_rows = [(row >= a * dh) & (row < (a + 1) * dh) for a in range(na)]

    km = km_ref[...]
    km2 = jnp.concatenate([jnp.where(head_lanes[a], km, 0.0) for a in range(na)], axis=0)
    p0 = km2.astype(BF16)
    r1 = km2 - p0.astype(F32)
    p1 = r1.astype(BF16)
    p2 = (r1 - p1.astype(F32)).astype(BF16)
    jidx = lax.broadcasted_iota(jnp.int32, (nb, bs), 0)
    kt = min(MOBA_TOPK, nb)

    def select(i, carry):
        qt = qt_ref[i]
        gate2 = _dot(p0, qt) + _dot(p1, qt) + _dot(p2, qt)
        past = jidx < i
        for a in range(na):
            qm_ref[i, a] = jnp.where(head_rows[a], qt, jnp.zeros_like(qt))
            gm = jnp.where(past, gate2[a * nb:(a + 1) * nb, :], MASK_VALUE * MOBA_SCALE)
            sel = jnp.where(jidx == i, 1.0, 0.0)
            for r in range(kt):
                top = jnp.max(gm, axis=0, keepdims=True)
                first = jnp.min(jnp.where(gm == top, jidx, nb), axis=0, keepdims=True)
                pick = jidx == first
                sel = jnp.maximum(sel, jnp.where(pick & past, 1.0, 0.0) * jnp.where(r < i, 1.0, 0.0))
                gm = jnp.where(pick, -jnp.inf, gm)
            sel_ref[i, a] = sel
        return carry

    lax.fori_loop(0, nb, select, 0, unroll=8)

    slopes = [slope_ref[na * hg + a] for a in range(na)]

    def block_ids(i, t):
        ids = []
        for u in range(nu):
            b = i - nu * t - u
            ids.append((jnp.maximum(b, 0), jnp.where(b >= 0, 1.0, 0.0)))
        return ids

    def scores(i, t):
        ids = block_ids(i, t)
        keys = jnp.concatenate([k_ref[pl.ds(pl.multiple_of(bc * bs, bs), bs), :] for bc, _ in ids], axis=0)
        qk = [_dot(keys, qm_ref[i, a]) for a in range(na)]
        s_all, stats = [], []
        for u, (bc, valid) in enumerate(ids):
            diag = jnp.where(bc == i, 1, 0)
            for a in range(na):
                s = qk[a][u * bs:(u + 1) * bs] + pb_ref[diag, a]
                chosen = sel_ref[i, a, pl.ds(bc, 1), :] * valid > 0.5
                off = jnp.where(chosen, slopes[a] * (bc.astype(F32) * float(bs)), -jnp.inf)
                s_all.append(s)
                stats.append((jnp.max(s, axis=0, keepdims=True) + off, off))
        return s_all, tuple(stats)

    ones_rows = jnp.ones((V_AUG_ROWS - dh, nu * bs), BF16)

    def step(n, carry):
        stats, state = carry
        i, t = tab_ref[0, n], tab_ref[1, n]
        s_cur = [s_ref[ua] for ua in range(nu * na)]
        s_nxt, stats_nxt = scores(tab_ref[0, n + 1], tab_ref[1, n + 1])
        ids = block_ids(i, t)
        first = t == 0
        cap = jnp.where(first, -jnp.inf, jnp.inf)
        keep = jnp.where(first, 0.0, 1.0)
        out = []
        for a in range(na):
            m_old = jnp.minimum(state[a][0], cap)
            m_new = m_old
            for u in range(nu):
                m_new = jnp.maximum(m_new, stats[u * na + a][0])
            p = [jnp.exp((s_cur[u * na + a] - (m_new - stats[u * na + a][1])).astype(BF16)) for u in range(nu)]
            v = jnp.concatenate([vt_ref[ids[u][0], a * dh:(a + 1) * dh, :] for u in range(nu)], axis=1)
            pv = _dot(jnp.concatenate([v, ones_rows], axis=0), jnp.concatenate(p, axis=0))
            out.append((m_new, (jnp.exp(m_old - m_new) * keep) * state[a][1] + pv))
        for ua in range(nu * na):
            s_ref[ua] = s_nxt[ua]

        for a in range(na):
            acc_ref[i, a] = out[a][1]
        return stats_nxt, tuple(out)

    s0, stats0 = scores(tab_ref[0, 0], tab_ref[1, 0])
    for ua in range(nu * na):
        s_ref[ua] = s0[ua]
    init = tuple((jnp.full((1, bs), -jnp.inf, F32), jnp.zeros((V_AUG_ROWS, bs), F32)) for _ in range(na))
    lax.fori_loop(0, n_steps, step, (stats0, init), unroll=8)

    def finalize(i, carry):
        o = [acc_ref[i, a, 0:dh] / acc_ref[i, a, dh:dh + 1] for a in range(na)]
        o_ref[pl.ds(pl.multiple_of(i * bs, bs), bs), :] = jnp.concatenate(o, axis=0).T.astype(o_ref.dtype)
        return carry

    lax.fori_loop(0, nb, finalize, 0, unroll=8)


def _moba(qt, k, vt, batch, seq):
    n = k.shape[0]
    bs, dh, nh, na = MOBA_BLOCK, MOBA_HEAD_DIM, MOBA_HEADS, MOBA_HEADS_PER_STEP
    nb = seq // bs
    dim = nh * dh
    slopes = jnp.exp2(-8.0 * jnp.arange(1, nh + 1, dtype=F32) / nh)
    key_pos = jnp.arange(bs, dtype=F32)[:, None]
    pos_bias = slopes[:, None, None] * jnp.broadcast_to(key_pos, (bs, bs))[None]
    causal = jnp.arange(bs)[:, None] <= jnp.arange(bs)[None, :]
    pos_bias = jnp.stack([pos_bias, jnp.where(causal[None], pos_bias, -jnp.inf)])
    nu = MOBA_BLOCKS_PER_STEP
    table = [(i, t) for i in range(nb) for t in range((i + nu) // nu)] + [(0, 0)]
    table = jnp.asarray(np.array(table, np.int32).T)
    smem = pl.BlockSpec(memory_space=pltpu.SMEM)
    transposed = pl.BlockSpec((nb, na * dh, bs), lambda b, hg: (b, hg, 0))
    return pl.pallas_call(
        functools.partial(_moba_kernel, nb=nb, n_steps=table.shape[1] - 1),
        grid=(batch, nh // na),
        in_specs=[smem, smem, transposed,
                  pl.BlockSpec((seq, na * dh), lambda b, hg: (b, hg)),
                  transposed,
                  pl.BlockSpec((2, na, bs, bs), lambda b, hg: (0, hg, 0, 0))],
        out_specs=pl.BlockSpec((seq, na * dh), lambda b, hg: (b, hg)),
        out_shape=jax.ShapeDtypeStruct((n, dim), BF16),
        scratch_shapes=[pltpu.VMEM((nb, na * dh), F32), pltpu.VMEM((nb, na, nb, bs), F32),
                        pltpu.VMEM((nb, na, na * dh, bs), BF16),
                        pltpu.VMEM((nu * na, bs, bs), F32), pltpu.VMEM((nb, na, V_AUG_ROWS, bs), F32)],
        compiler_params=_params("parallel", "parallel"),
        name="moba",
    )(slopes, table, qt, k, vt, pos_bias)


def kernel(x, ffn1_norm, ffn1_w_gu, ffn1_w_down, mix_norm, w_in, hgrn_lb_logits, hgrn_norm,
           w_branch_a, w_branch_b, w_out, ffn2_norm, ffn2_w_gu, ffn2_w_down, final_norm):
    batch, seq, d = x.shape
    depth = ffn1_norm.shape[0]
    hdim = 4 * HGRN_HEADS * HGRN_DK
    mdim = MOBA_HEADS * MOBA_HEAD_DIM
    xf = x.reshape(batch * seq, d)
    w1gu, w1d, w2gu, w2d, wi, wa, wb, wo = (_to_bf16(w) for w in (
        ffn1_w_gu, ffn1_w_down, ffn2_w_gu, ffn2_w_down, w_in, w_branch_a, w_branch_b, w_out))
    for l in range(depth):
        xf = _ffn(xf, ffn1_norm[l], w1gu, w1d, l)
        w_q = wi[l, :, hdim:hdim + mdim]
        w_v = wi[l, :, hdim + 2 * mdim:hdim + 3 * mdim]
        h4, k, qt, vt, gates = _inproj(xf, mix_norm[l], wi, jnp.concatenate([w_q, w_v], axis=1).T, l)
        oa = _hgrn(h4, hgrn_lb_logits, hgrn_norm[l], l, batch, seq)
        ob = _moba(qt, k, vt, batch, seq)
        xf = _ffn(xf, ffn2_norm[l], w2gu, w2d, l, merge=(oa, ob, gates, wa, wb, wo),
                  final_w=final_norm if l == depth - 1 else None)
    return xf.reshape(batch, seq, d)
```

```python
import functools

import numpy as np
import jax
import jax.numpy as jnp
from jax import lax
from jax.experimental import pallas as pl
from jax.experimental.pallas import tpu as pltpu

F32 = jnp.float32
BF16 = jnp.bfloat16

NORM_EPS = 1e-6
MIN_FORGET = 1e-20
MASK_VALUE = -1e9

HGRN_HEADS = 4
HGRN_DK = 128
MOBA_HEADS = 8
MOBA_HEAD_DIM = 64
MOBA_BLOCK = 256
MOBA_TOPK = 3
MOBA_SCALE = MOBA_HEAD_DIM ** -0.5

V7X_VMEM_LIMIT_BYTES = 56 * 1024 * 1024

LOG2_E = 1.4426950408889634
HGRN_CHUNK = 128
HGRN_MATMUL_LEVEL_LIMIT = 8
HGRN_ROWS_PER_STEP = 1024
V_AUG_ROWS = 80
MOBA_HEADS_PER_STEP = 4
MOBA_BLOCKS_PER_STEP = 2


def _rmsnorm(x, w):
    return x * lax.rsqrt(jnp.mean(x * x, axis=-1, keepdims=True) + NORM_EPS) * w


def _silu(x):
    return x * jax.nn.sigmoid(x)


def _dot(a, b):
    return jnp.dot(a, b, preferred_element_type=F32)


def _dot_nt(a, b):
    return lax.dot_general(a, b, (((1,), (1,)), ((), ())), preferred_element_type=F32)


def _resident(shape):
    nd = len(shape)
    return pl.BlockSpec(shape, lambda *_: (0,) * nd, pipeline_mode=pl.Buffered(1))


def _resident_layer(stacked, layer):
    return pl.BlockSpec((None,) + stacked.shape[1:], lambda *_: (layer, 0, 0), pipeline_mode=pl.Buffered(1))


def _cast_kernel(w_ref, o_ref):
    o_ref[...] = w_ref[...].astype(o_ref.dtype)


def _to_bf16(w):
    depth, r, c = w.shape
    rows = max(k for k in range(16, r + 1, 16) if r % k == 0 and k * c * 4 <= 6 * 1024 * 1024)
    block = pl.BlockSpec((1, rows, c), lambda l, i: (l, i, 0))
    return pl.pallas_call(
        _cast_kernel,
        grid=(depth, r // rows),
        in_specs=[block],
        out_specs=block,
        out_shape=jax.ShapeDtypeStruct(w.shape, BF16),
        compiler_params=_params("parallel", "parallel"),
        name="to_bf16",
    )(w)


def _params(*sem):
    return pltpu.CompilerParams(dimension_semantics=sem, vmem_limit_bytes=V7X_VMEM_LIMIT_BYTES)


def _ffn_kernel(x_ref, *refs, merge, final):
    refs = list(refs)
    o_ref = refs.pop()
    parts = 1 if merge else 2
    rows = x_ref.shape[0] // parts
    for rs in [slice(j * rows, (j + 1) * rows) for j in range(parts)]:
        rest = refs
        x = x_ref[rs, :]
        if merge:
            oa_ref, ob_ref, g_ref, wa_ref, wb_ref, wo_ref = rest[:6]
            rest = rest[6:]
            d = x.shape[1]
            ga = jax.nn.sigmoid(g_ref[rs, 0:d].astype(F32))
            gb = jax.nn.sigmoid(g_ref[rs, d:2 * d].astype(F32))
            merged = ga * _dot(oa_ref[rs, :], wa_ref[...]) + gb * _dot(ob_ref[rs, :], wb_ref[...])
            x = x + _dot(merged.astype(BF16), wo_ref[...])
        nw_ref, wgu_ref, wd_ref = rest[:3]
        ff = wd_ref.shape[0]
        h = _rmsnorm(x, nw_ref[...]).astype(BF16)
        g = _dot(h, wgu_ref[:, :ff])
        u = _dot(h, wgu_ref[:, ff:])
        a = (_silu(g) * u).astype(BF16)
        y = x + 0.5 * _dot(a, wd_ref[...])
        if final:
            y = _rmsnorm(y, rest[3][...])
        o_ref[rs, :] = y


def _ffn(x, nw, wgu, wd, layer, merge=None, final_w=None, tm=512):
    n, d = x.shape

    def row(width):
        return pl.BlockSpec((tm, width), lambda i: (i, 0))

    in_specs, args = [row(d)], [x]
    if merge is not None:
        in_specs += [row(a.shape[1]) for a in merge[:3]] + [_resident_layer(a, layer) for a in merge[3:]]
        args += list(merge)
    in_specs += [_resident((1, d)), _resident_layer(wgu, layer), _resident_layer(wd, layer)]
    args += [nw.reshape(1, d), wgu, wd]
    if final_w is not None:
        in_specs.append(_resident((1, d)))
        args.append(final_w.reshape(1, d))
    return pl.pallas_call(
        functools.partial(_ffn_kernel, merge=merge is not None, final=final_w is not None),
        grid=(n // tm,),
        in_specs=in_specs,
        out_specs=row(d),
        out_shape=jax.ShapeDtypeStruct((n, d), F32),
        compiler_params=_params("parallel"),
        name="ffn",
    )(*args)


def _inproj_kernel(x_ref, nw_ref, wi_ref, wqvt_ref, h_ref, k_ref, qt_ref, vt_ref, g_ref):
    dh, dk, dg = h_ref.shape[1], k_ref.shape[1], g_ref.shape[1]
    k0 = dh + dk
    dq = qt_ref.shape[1]
    for blk in range(qt_ref.shape[0]):
        rs = slice(blk * MOBA_BLOCK, (blk + 1) * MOBA_BLOCK)
        h = _rmsnorm(x_ref[rs, :], nw_ref[...]).astype(BF16)
        h_ref[rs, :] = _dot(h, wi_ref[:, :dh])
        k_ref[rs, :] = _dot(h, wi_ref[:, k0:k0 + dk]).astype(BF16)
        qvt = _dot_nt(wqvt_ref[...], h)
        qt_ref[blk] = (qvt[:dq] * MOBA_SCALE).astype(BF16)
        vt_ref[blk] = qvt[dq:].astype(BF16)
        g_ref[rs, :] = _dot(h, wi_ref[:, wi_ref.shape[1] - dg:]).astype(BF16)


def _inproj(x, nw, wi, wqvt, layer, tm=512):
    n, d = x.shape
    dqv = wqvt.shape[0]
    dk = dqv // 2
    dh = 4 * HGRN_HEADS * HGRN_DK
    dg = wi.shape[2] - dh - 3 * dk
    blocks = tm // MOBA_BLOCK
    transposed = pl.BlockSpec((blocks, dqv // 2, MOBA_BLOCK), lambda i: (i, 0, 0))
    return pl.pallas_call(
        _inproj_kernel,
        grid=(n // tm,),
        in_specs=[pl.BlockSpec((tm, d), lambda i: (i, 0)), _resident((1, d)),
                  _resident_layer(wi, layer), _resident((dqv, d))],
        out_specs=[pl.BlockSpec((tm, dh), lambda i: (i, 0)),
                   pl.BlockSpec((tm, dk), lambda i: (i, 0)),
                   transposed, transposed,
                   pl.BlockSpec((tm, dg), lambda i: (i, 0))],
        out_shape=[jax.ShapeDtypeStruct((n, dh), F32),
                   jax.ShapeDtypeStruct((n, dk), BF16),
                   jax.ShapeDtypeStruct((n // MOBA_BLOCK, dqv // 2, MOBA_BLOCK), BF16),
                   jax.ShapeDtypeStruct((n // MOBA_BLOCK, dqv // 2, MOBA_BLOCK), BF16),
                   jax.ShapeDtypeStruct((n, dg), BF16)],
        compiler_params=_params("parallel"),
        name="inproj",
    )(x, nw.reshape(1, d), wi, wqvt)


def _hgrn_constants(c):
    t = np.arange(c)
    sp = t[None, :]
    tt = t[:, None]
    rows = [sp <= tt, sp > tt]
    masks, upper = [], []
    m = 1
    while m < c:
        up = (t // m) % 2 == 1
        bnd = ((t // (2 * m)) * 2 * m + m - 1)[:, None]
        if m < HGRN_MATMUL_LEVEL_LIMIT:
            rows.append(np.where(up[:, None], (sp > bnd) & (sp <= tt), (sp > tt) & (sp <= bnd)))
        same = (tt // (2 * m)) == (sp // (2 * m))
        masks.append(up[:, None] & (~up)[None, :] & same)
        upper.append(np.broadcast_to(up[:, None], (c, HGRN_DK)))
        m *= 2
    w = np.concatenate(rows, axis=0).astype(np.float32)
    w = np.concatenate([w, w], axis=1)
    return (jnp.asarray(w, BF16), jnp.asarray(np.stack(masks), F32),
            jnp.asarray(np.stack(upper), F32))


def _hgrn_kernel(lbl_ref, nw_ref, w_ref, lm_ref, up_ref, h_ref, o_ref, st_ref, *, layer, n_levels):
    c = HGRN_CHUNK
    dk = HGRN_DK
    kdim = HGRN_HEADS * dk

    @pl.when(pl.program_id(1) == 0)
    def _():
        st_ref[...] = jnp.zeros_like(st_ref)

    logits = lbl_ref[...]
    e = jnp.exp(logits - jnp.max(logits, axis=0, keepdims=True))
    sm = e / jnp.sum(e, axis=0, keepdims=True)
    cum = sm[0:1, :]
    for j in range(1, layer + 1):
        cum = cum + sm[j:j + 1, :]
    lb_all = jnp.clip(cum - sm[0:1, :], 0.0, 1.0)
    w = w_ref[...]
    nw = nw_ref[...]

    heads = [slice(h * dk, (h + 1) * dk) for h in range(HGRN_HEADS)]

    def chunk(ci, carry):
        r0 = pl.multiple_of(ci * c, c)
        q = _silu(h_ref[pl.ds(r0, c), 0:kdim])
        sig = jax.nn.sigmoid(h_ref[pl.ds(r0, c), kdim:2 * kdim])
        inp = h_ref[pl.ds(r0, c), 2 * kdim:3 * kdim]
        og = h_ref[pl.ds(r0, c), 3 * kdim:4 * kdim]
        f = lb_all + (1.0 - lb_all) * sig
        g = jnp.log(jnp.maximum(f, MIN_FORGET)) * LOG2_E
        k = (1.0 - lb_all) * (1.0 - sig)

        g_hi = g.astype(BF16)
        g_lo = (g - g_hi.astype(F32)).astype(BF16)
        e = _dot(w, jnp.concatenate([g_hi, g_lo], axis=0))
        x = jnp.exp2(e)
        b = e[0:c]

        qd = (q * x[0:c]).astype(BF16)
        kd = (k * x[c:2 * c]).astype(BF16)
        decay_all = x[c - 1:c, :]
        dqk = q - k
        a = [jnp.zeros((c, c), F32) for _ in heads]
        for l in range(n_levels):
            m = 2 ** l
            if m < HGRN_MATMUL_LEVEL_LIMIT:
                up = jnp.concatenate([up_ref[l]] * HGRN_HEADS, axis=1)
                z = ((k + up * dqk) * x[(2 + l) * c:(3 + l) * c]).astype(BF16)
            else:
                b3 = b.reshape(c // (2 * m), 2 * m, kdim)
                d = (b3 - b3[:, m - 1:m, :]).reshape(c, kdim)
                blocks = [slice(j * m, (j + 1) * m) for j in range(c // m)]
                e_l = jnp.concatenate([d[rs] if j % 2 else -d[rs] for j, rs in enumerate(blocks)], axis=0)
                qk_rows = jnp.concatenate([q[rs] if j % 2 else k[rs] for j, rs in enumerate(blocks)], axis=0)
                z = (qk_rows * jnp.exp2(e_l)).astype(BF16)
            mask = lm_ref[l] > 0.5
            a = [jnp.where(mask, _dot_nt(z[:, hs], z[:, hs]), a[h]) for h, hs in enumerate(heads)]
        inp_b = inp.astype(BF16)
        qk = q * k
        outs = []
        for h, hs in enumerate(heads):
            st = st_ref[h]
            o = (_dot(a[h].astype(BF16), inp_b[:, hs]) + _dot_nt(qd[:, hs], st.astype(BF16))
                 + jnp.sum(qk[:, hs], axis=-1, keepdims=True) * inp[:, hs])
            st_ref[h] = st * decay_all[:, hs] + _dot(inp[:, hs].T.astype(BF16), kd[:, hs])
            outs.append(o * lax.rsqrt(jnp.mean(o * o, axis=-1, keepdims=True) + NORM_EPS) * nw)
        o_ref[pl.ds(r0, c), :] = (jnp.concatenate(outs, axis=1) * _silu(og)).astype(o_ref.dtype)
        return carry

    lax.fori_loop(0, h_ref.shape[0] // c, chunk, 0, unroll=True)


def _hgrn(h4, lb_logits, norm_w, layer, batch, seq):
    n = h4.shape[0]
    c = HGRN_CHUNK
    tb = HGRN_ROWS_PER_STEP
    steps = seq // tb
    w, lm, up = _hgrn_constants(c)
    n_levels = lm.shape[0]
    vdim = HGRN_HEADS * HGRN_DK
    return pl.pallas_call(
        functools.partial(_hgrn_kernel, layer=layer, n_levels=n_levels),
        grid=(batch, steps),
        in_specs=[_resident(lb_logits.shape), _resident((1, HGRN_DK)), _resident(w.shape),
                  _resident(lm.shape), _resident(up.shape),
                  pl.BlockSpec((tb, h4.shape[1]), lambda b, s: (b * steps + s, 0))],
        out_specs=pl.BlockSpec((tb, vdim), lambda b, s: (b * steps + s, 0)),
        out_shape=jax.ShapeDtypeStruct((n, vdim), BF16),
        scratch_shapes=[pltpu.VMEM((HGRN_HEADS, HGRN_DK, HGRN_DK), F32)],
        compiler_params=_params("parallel", "arbitrary"),
        name="hgrn",
    )(lb_logits, norm_w.reshape(1, HGRN_DK), w, lm, up, h4)


def _moba_kernel(slope_ref, tab_ref, qt_ref, k_ref, vt_ref, pb_ref, o_ref, km_ref, sel_ref, qm_ref, s_ref, acc_ref, *, nb, n_steps):
    bs = MOBA_BLOCK
    dh = MOBA_HEAD_DIM
    na = MOBA_HEADS_PER_STEP
    nu = MOBA_BLOCKS_PER_STEP
    width = na * dh
    hg = pl.program_id(1)

    for j in range(nb):
        km_ref[j:j + 1, :] = jnp.mean(k_ref[j * bs:(j + 1) * bs, :].astype(F32), axis=0, keepdims=True)

    lane = lax.broadcasted_iota(jnp.int32, (1, width), 1)
    row = lax.broadcasted_iota(jnp.int32, (width, 1), 0)
    head_lanes = [(lane >= a * dh) & (lane < (a + 1) * dh) for a in range(na)]
    head_rows = [(row >= a * dh) & (row < (a + 1) * dh) for a in range(na)]

    km = km_ref[...]
    km2 = jnp.concatenate([jnp.where(head_lanes[a], km, 0.0) for a in range(na)], axis=0)
    p0 = km2.astype(BF16)
    r1 = km2 - p0.astype(F32)
    p1 = r1.astype(BF16)
    p2 = (r1 - p1.astype(F32)).astype(BF16)
    jidx = lax.broadcasted_iota(jnp.int32, (nb, bs), 0)
    kt = min(MOBA_TOPK, nb)

    def select(i, carry):
        qt = qt_ref[i]
        gate2 = _dot(p0, qt) + _dot(p1, qt) + _dot(p2, qt)
        past = jidx < i
        for a in range(na):
            qm_ref[i, a] = jnp.where(head_rows[a], qt, jnp.zeros_like(qt))
            gm = jnp.where(past, gate2[a * nb:(a + 1) * nb, :], MASK_VALUE * MOBA_SCALE)
            sel = jnp.where(jidx == i, 1.0, 0.0)
            for r in range(kt):
                top = jnp.max(gm, axis=0, keepdims=True)
                first = jnp.min(jnp.where(gm == top, jidx, nb), axis=0, keepdims=True)
                pick = jidx == first
                sel = jnp.maximum(sel, jnp.where(pick & past, 1.0, 0.0) * jnp.where(r < i, 1.0, 0.0))
                gm = jnp.where(pick, -jnp.inf, gm)
            sel_ref[i, a] = sel
        return carry

    lax.fori_loop(0, nb, select, 0, unroll=8)

    slopes = [slope_ref[na * hg + a] for a in range(na)]

    def block_ids(i, t):
        ids = []
        for u in range(nu):
            b = i - nu * t - u
            ids.append((jnp.maximum(b, 0), jnp.where(b >= 0, 1.0, 0.0)))
        return ids

    def scores(i, t):
        ids = block_ids(i, t)
        keys = jnp.concatenate([k_ref[pl.ds(pl.multiple_of(bc * bs, bs), bs), :] for bc, _ in ids], axis=0)
        qk = [_dot(keys, qm_ref[i, a]) for a in range(na)]
        s_all, stats = [], []
        for u, (bc, valid) in enumerate(ids):
            diag = jnp.where(bc == i, 1, 0)
            for a in range(na):
                s = qk[a][u * bs:(u + 1) * bs] + pb_ref[diag, a]
                chosen = sel_ref[i, a, pl.ds(bc, 1), :] * valid > 0.5
                off = jnp.where(chosen, slopes[a] * (bc.astype(F32) * float(bs)), -jnp.inf)
                s_all.append(s)
                stats.append((jnp.max(s, axis=0, keepdims=True) + off, off))
        return s_all, tuple(stats)

    ones_rows = jnp.ones((V_AUG_ROWS - dh, nu * bs), BF16)

    def step(n, carry):
        stats, state = carry
        i, t = tab_ref[0, n], tab_ref[1, n]
        s_cur = [s_ref[ua] for ua in range(nu * na)]
        s_nxt, stats_nxt = scores(tab_ref[0, n + 1], tab_ref[1, n + 1])
        ids = block_ids(i, t)
        first = t == 0
        cap = jnp.where(first, -jnp.inf, jnp.inf)
        keep = jnp.where(first, 0.0, 1.0)
        out = []
        for a in range(na):
            m_old = jnp.minimum(state[a][0], cap)
            m_new = m_old
            for u in range(nu):
                m_new = jnp.maximum(m_new, stats[u * na + a][0])
            p = [jnp.exp((s_cur[u * na + a] - (m_new - stats[u * na + a][1])).astype(BF16)) for u in range(nu)]
            v = jnp.concatenate([vt_ref[ids[u][0], a * dh:(a + 1) * dh, :] for u in range(nu)], axis=1)
            pv = _dot(jnp.concatenate([v, ones_rows], axis=0), jnp.concatenate(p, axis=0))
            out.append((m_new, (jnp.exp(m_old - m_new) * keep) * state[a][1] + pv))
        for ua in range(nu * na):
            s_ref[ua] = s_nxt[ua]

        for a in range(na):
            acc_ref[i, a] = out[a][1]
        return stats_nxt, tuple(out)

    s0, stats0 = scores(tab_ref[0, 0], tab_ref[1, 0])
    for ua in range(nu * na):
        s_ref[ua] = s0[ua]
    init = tuple((jnp.full((1, bs), -jnp.inf, F32), jnp.zeros((V_AUG_ROWS, bs), F32)) for _ in range(na))
    lax.fori_loop(0, n_steps, step, (stats0, init), unroll=24)

    def finalize(i, carry):
        o = [acc_ref[i, a, 0:dh] / acc_ref[i, a, dh:dh + 1] for a in range(na)]
        o_ref[pl.ds(pl.multiple_of(i * bs, bs), bs), :] = jnp.concatenate(o, axis=0).T.astype(o_ref.dtype)
        return carry

    lax.fori_loop(0, nb, finalize, 0, unroll=8)


def _moba(qt, k, vt, batch, seq):
    n = k.shape[0]
    bs, dh, nh, na = MOBA_BLOCK, MOBA_HEAD_DIM, MOBA_HEADS, MOBA_HEADS_PER_STEP
    nb = seq // bs
    dim = nh * dh
    slopes = jnp.exp2(-8.0 * jnp.arange(1, nh + 1, dtype=F32) / nh)
    key_pos = jnp.arange(bs, dtype=F32)[:, None]
    pos_bias = slopes[:, None, None] * jnp.broadcast_to(key_pos, (bs, bs))[None]
    causal = jnp.arange(bs)[:, None] <= jnp.arange(bs)[None, :]
    pos_bias = jnp.stack([pos_bias, jnp.where(causal[None], pos_bias, -jnp.inf)])
    nu = MOBA_BLOCKS_PER_STEP
    table = [(i, t) for i in range(nb) for t in range((i + nu) // nu)] + [(0, 0)]
    table = jnp.asarray(np.array(table, np.int32).T)
    smem = pl.BlockSpec(memory_space=pltpu.SMEM)
    transposed = pl.BlockSpec((nb, na * dh, bs), lambda b, hg: (b, hg, 0))
    return pl.pallas_call(
        functools.partial(_moba_kernel, nb=nb, n_steps=table.shape[1] - 1),
        grid=(batch, nh // na),
        in_specs=[smem, smem, transposed,
                  pl.BlockSpec((seq, na * dh), lambda b, hg: (b, hg)),
                  transposed,
                  pl.BlockSpec((2, na, bs, bs), lambda b, hg: (0, hg, 0, 0))],
        out_specs=pl.BlockSpec((seq, na * dh), lambda b, hg: (b, hg)),
        out_shape=jax.ShapeDtypeStruct((n, dim), BF16),
        scratch_shapes=[pltpu.VMEM((nb, na * dh), F32), pltpu.VMEM((nb, na, nb, bs), F32),
                        pltpu.VMEM((nb, na, na * dh, bs), BF16),
                        pltpu.VMEM((nu * na, bs, bs), F32), pltpu.VMEM((nb, na, V_AUG_ROWS, bs), F32)],
        compiler_params=_params("parallel", "parallel"),
        name="moba",
    )(slopes, table, qt, k, vt, pos_bias)


def kernel(x, ffn1_norm, ffn1_w_gu, ffn1_w_down, mix_norm, w_in, hgrn_lb_logits, hgrn_norm,
           w_branch_a, w_branch_b, w_out, ffn2_norm, ffn2_w_gu, ffn2_w_down, final_norm):
    batch, seq, d = x.shape
    depth = ffn1_norm.shape[0]
    hdim = 4 * HGRN_HEADS * HGRN_DK
    mdim = MOBA_HEADS * MOBA_HEAD_DIM
    xf = x.reshape(batch * seq, d)
    w1gu, w1d, w2gu, w2d, wi, wa, wb, wo = (_to_bf16(w) for w in (
        ffn1_w_gu, ffn1_w_down, ffn2_w_gu, ffn2_w_down, w_in, w_branch_a, w_branch_b, w_out))
    for l in range(depth):
        xf = _ffn(xf, ffn1_norm[l], w1gu, w1d, l)
        w_q = wi[l, :, hdim:hdim + mdim]
        w_v = wi[l, :, hdim + 2 * mdim:hdim + 3 * mdim]
        h4, k, qt, vt, gates = _inproj(xf, mix_norm[l], wi, jnp.concatenate([w_q, w_v], axis=1).T, l)
        oa = _hgrn(h4, hgrn_lb_logits, hgrn_norm[l], l, batch, seq)
        ob = _moba(qt, k, vt, batch, seq)
        xf = _ffn(xf, ffn2_norm[l], w2gu, w2d, l, merge=(oa, ob, gates, wa, wb, wo),
                  final_w=final_norm if l == depth - 1 else None)
    return xf.reshape(batch, seq, d)
```
